```python
import jax
import jax.numpy as jnp
from jax import lax
import numpy as np

D_MODEL = 2048
BATCH = 32
SEQ = 256
DEPTH = 2
DEC_BATCH = 8
DEC_SEQ = 1024
PAST_LEN = 256

GRID_W = 64
HG_HEADS = 8
HG_DK = 128
HG_DV = 128
HG_WIDTH = HG_HEADS * HG_DV
HG_CHUNK = 16
NA_HEADS = 8
NA_DH = 128
NA_WIDTH = NA_HEADS * NA_DH
NA_WIN_R = 8
NA_WIN_C = 16
NA_QB = 16
NA_KC = NA_QB + NA_WIN_C
NA_NCB = GRID_W // NA_QB
CTX_QB = 128
D_FF = 5632
N_EXPERTS = 8
TOP_K = 2
D_FF_E = 7168
N_DENSE = (DEPTH + 1) // 2
N_MOE = DEPTH // 2
IN_COLS = 5 * HG_WIDTH + 3 * NA_WIDTH + 2 * D_MODEL
NORM_EPS = 1e-6
NEG_INF = -1e30

kernel_name = 'hybrid_hgrn2_natten_flow_step'


def rmsnorm(x, g):
    xf = x.astype(jnp.float32)
    y = xf * lax.rsqrt(jnp.mean(xf * xf, axis=-1, keepdims=True) + NORM_EPS)
    return (y * g.astype(jnp.float32)).astype(x.dtype)


def to_heads(x, n):
    b, t, w = x.shape
    return x.reshape(b, t, n, w // n).transpose(0, 2, 1, 3)


def from_heads(x):
    b, h, t, d = x.shape
    return x.transpose(0, 2, 1, 3).reshape(b, t, h * d)


def modulation(cond, w_ada, b_ada):
    m = jax.nn.silu(cond) @ w_ada + b_ada
    return jnp.split(m, 6, axis=-1)


def split_mixer_inputs(h, w_in):
    z = h @ w_in
    hw, nw = HG_WIDTH, NA_WIDTH
    cuts = [hw, 2 * hw, 3 * hw, 4 * hw, 5 * hw, 5 * hw + nw, 5 * hw + 2 * nw, 5 * hw + 3 * nw,
            5 * hw + 3 * nw + D_MODEL]
    return jnp.split(z, cuts, axis=-1)


def hgrn_lower_bounds(hg_lb):
    p = jax.nn.softmax(hg_lb.astype(jnp.float32), axis=1)
    cs = jnp.cumsum(p, axis=1)
    return cs - cs[:, :1]


def hgrn_gates(z, lb):
    zf = z.astype(jnp.float32)
    lbf = lb.astype(jnp.float32)
    log_f = jnp.logaddexp(jnp.log(lbf), jnp.log1p(-lbf) + jax.nn.log_sigmoid(zf))
    k = (1.0 - lbf) * jax.nn.sigmoid(-zf)
    return to_heads(log_f, HG_HEADS), to_heads(k, HG_HEADS)


def hgrn_chunk_scan(q, k, v, log_f, s0):
    b, h, t, dk = q.shape
    dv = v.shape[-1]
    n = t // HG_CHUNK
    q = q.astype(jnp.float32).reshape(b, h, n, HG_CHUNK, dk)
    k = k.astype(jnp.float32).reshape(b, h, n, HG_CHUNK, dk)
    v = v.astype(jnp.float32).reshape(b, h, n, HG_CHUNK, dv)
    cum = jnp.cumsum(log_f.astype(jnp.float32).reshape(b, h, n, HG_CHUNK, dk), axis=3)
    cum_last = cum[:, :, :, -1:, :]
    q_t = q * jnp.exp(cum)
    k_t = k * jnp.exp(-cum)
    k_end = k * jnp.exp(cum_last - cum)
    causal = jnp.tril(jnp.ones((HG_CHUNK, HG_CHUNK), dtype=bool))
    a = jnp.where(causal, jnp.einsum('bhncd,bhnsd->bhncs', q_t, k_t), 0.0)
    o_intra = jnp.einsum('bhncs,bhnse->bhnce', a, v)
    ds = jnp.einsum('bhncd,bhnce->nbhde', k_end, v)
    decay = jnp.exp(cum_last[:, :, :, 0, :]).transpose(2, 0, 1, 3)

    def step(s, inp):
        dec, d = inp
        return dec[..., None] * s + d, s

    s_fin, s_prev = lax.scan(step, s0.astype(jnp.float32), (decay, ds))
    o_inter = jnp.einsum('bhncd,nbhde->bhnce', q_t, s_prev)
    return (o_intra + o_inter).reshape(b, h, t, dv), s_fin


def hgrn_branch(hq, hf_fwd, hf_bwd, hi, hg, lb_fwd, lb_bwd, s0_fwd, s0_bwd, onorm_g):
    q = to_heads(jax.nn.silu(hq), HG_HEADS)
    v = to_heads(hi, HG_HEADS)
    logf_f, k_f = hgrn_gates(hf_fwd, lb_fwd)
    o_f, s_f = hgrn_chunk_scan(q, k_f, v, logf_f, s0_fwd)
    logf_b, k_b = hgrn_gates(hf_bwd, lb_bwd)
    rev = lambda a: jnp.flip(a, axis=2)
    o_b, s_b = hgrn_chunk_scan(rev(q), rev(k_b), rev(v), rev(logf_b), s0_bwd)
    o = rmsnorm(o_f + rev(o_b), onorm_g)
    y = from_heads(o).astype(hq.dtype) * jax.nn.silu(hg)
    return y, jnp.stack([s_f, s_b], axis=1)


def qk_heads(nq, nk, nv, qn_g, kn_g):
    q = rmsnorm(to_heads(nq, NA_HEADS), qn_g)
    k = rmsnorm(to_heads(nk, NA_HEADS), kn_g)
    return q, k, to_heads(nv, NA_HEADS)


def context_attention(q, k, v):
    b, h, s, dh = q.shape
    nb = s // CTX_QB
    q_blocks = q.reshape(b, h, nb, CTX_QB, dh).transpose(2, 0, 1, 3, 4)
    scale = NA_DH ** -0.5

    def block(qi):
        logits = jnp.einsum('bhqd,bhkd->bhqk', qi, k).astype(jnp.float32) * scale
        p = jax.nn.softmax(logits, axis=-1).astype(v.dtype)
        return jnp.einsum('bhqk,bhkd->bhqd', p, v)

    o = lax.map(block, q_blocks)
    return o.transpose(1, 2, 0, 3, 4).reshape(b, h, s, dh)


def neighbourhood_attention(q, k, v, k_ctx, v_ctx, rpb):
    b, h, t, dh = q.shape
    rows = t // GRID_W
    wr = min(NA_WIN_R, rows)
    nkw = wr * NA_KC
    col_q = np.arange(GRID_W).reshape(NA_NCB, NA_QB)
    band0 = np.clip(np.arange(NA_NCB) * NA_QB - NA_WIN_C // 2, 0, GRID_W - NA_KC)
    col_k = band0[:, None] + np.arange(NA_KC)[None, :]
    q_start = np.clip(col_q - NA_WIN_C // 2, 0, GRID_W - NA_WIN_C)
    from_start = col_k[:, None, :] - q_start[:, :, None]
    col_valid = (from_start >= 0) & (from_start < NA_WIN_C)
    col_off = np.clip(col_k[:, None, :] - col_q[:, :, None] + NA_WIN_C - 1, 0, 2 * NA_WIN_C - 2)
    valid = np.broadcast_to(col_valid[:, :, None, :], (NA_NCB, NA_QB, wr, NA_KC)).reshape(NA_NCB, NA_QB, nkw)
    k_cols = k.reshape(b, h, rows, GRID_W, dh)[:, :, :, col_k]
    v_cols = v.reshape(b, h, rows, GRID_W, dh)[:, :, :, col_k]
    col_bias = rpb.astype(jnp.float32)[:, :, col_off]
    q_rows = q.reshape(b, h, rows, NA_NCB, NA_QB, dh).transpose(2, 0, 1, 3, 4, 5)
    scale = NA_DH ** -0.5

    def row_block(args):
        r, q_r = args
        r0 = jnp.clip(r - wr // 2, 0, rows - wr)
        k_win = lax.dynamic_slice_in_dim(k_cols, r0, wr, axis=2).transpose(0, 1, 3, 2, 4, 5).reshape(b, h, NA_NCB, nkw, dh)
        v_win = lax.dynamic_slice_in_dim(v_cols, r0, wr, axis=2).transpose(0, 1, 3, 2, 4, 5).reshape(b, h, NA_NCB, nkw, dh)
        row_idx = r0 - r + jnp.arange(wr) + NA_WIN_R - 1
        bias = col_bias[:, row_idx].transpose(0, 2, 3, 1, 4).reshape(h, NA_NCB, NA_QB, nkw)
        bias = jnp.where(valid, bias, NEG_INF)
        s_win = jnp.einsum('bhnqd,bhnkd->bhnqk', q_r, k_win).astype(jnp.float32) * scale + bias
        s_ctx = jnp.einsum('bhnqd,bhcd->bhnqc', q_r, k_ctx).astype(jnp.float32) * scale
        p = jax.nn.softmax(jnp.concatenate([s_win, s_ctx], axis=-1), axis=-1).astype(v.dtype)
        return (jnp.einsum('bhnqk,bhnkd->bhnqd', p[..., :nkw], v_win)
                + jnp.einsum('bhnqc,bhcd->bhnqd', p[..., nkw:], v_ctx))

    o = lax.map(row_block, (jnp.arange(rows), q_rows))
    return o.transpose(1, 2, 0, 3, 4, 5).reshape(b, h, t, dh)


def merge_branches(ya, yb, ga, gb, w_hb, w_nb, w_out):
    m = jax.nn.sigmoid(ga) * (ya @ w_hb) + jax.nn.sigmoid(gb) * (yb @ w_nb)
    return m @ w_out


def swiglu(h, wg, wu, wd):
    return (jax.nn.silu(h @ wg) * (h @ wu)) @ wd


def moe_swiglu(h, router, wg, wu, wd):
    logits = (h @ router).astype(jnp.float32)
    top_v, top_i = lax.top_k(logits, TOP_K)
    top_w = jax.nn.softmax(top_v, axis=-1)
    gate = jnp.sum(jax.nn.one_hot(top_i, N_EXPERTS, dtype=jnp.float32) * top_w[..., None], axis=-2)
    y = jnp.zeros_like(h)
    for e in range(N_EXPERTS):
        y = y + gate[..., e:e + 1].astype(h.dtype) * swiglu(h, wg[e], wu[e], wd[e])
    return y


def channel_mixer(h, l, ffn_wg, ffn_wu, ffn_wd, moe_router, moe_wg, moe_wu, moe_wd):
    i = l // 2
    if l % 2 == 0:
        return swiglu(h, ffn_wg[i], ffn_wu[i], ffn_wd[i])
    return moe_swiglu(h, moe_router[i], moe_wg[i], moe_wu[i], moe_wd[i])


def setup_inputs(seed: int = 0) -> dict:
    key = jax.random.key(seed)
    ks = iter(jax.random.split(key, 32))
    nrm = lambda shape, s: jax.random.normal(next(ks), shape, jnp.float32) * s
    d = D_MODEL
    return {
        'x_prompt': nrm((BATCH, SEQ, d), 1.0),
        'x_sample': nrm((DEC_BATCH, DEC_SEQ, d), 1.0),
        'cache_k': nrm((DEC_BATCH, DEPTH, NA_HEADS, PAST_LEN, NA_DH), 1.0),
        'cache_v': nrm((DEC_BATCH, DEPTH, NA_HEADS, PAST_LEN, NA_DH), 1.0),
        'state_hgrn': nrm((DEC_BATCH, DEPTH, 2, HG_HEADS, HG_DK, HG_DV), 0.5),
        'c': nrm((DEC_BATCH, d), 1.0),
        'c_ctx': nrm((d,), 1.0),
        'norm1_g': 1.0 + nrm((DEPTH, d), 0.02),
        'norm2_g': 1.0 + nrm((DEPTH, d), 0.02),
        'w_ada': nrm((DEPTH, d, 6 * d), 0.5 * d ** -0.5),
        'b_ada': nrm((DEPTH, 6 * d), 0.02),
        'w_in': nrm((DEPTH, d, IN_COLS), d ** -0.5),
        'hg_lb': nrm((2, DEPTH, HG_WIDTH), 0.5),
        'hg_onorm_g': 1.0 + nrm((DEPTH, HG_DV), 0.02),
        'na_qn_g': 1.0 + nrm((DEPTH, NA_DH), 0.02),
        'na_kn_g': 1.0 + nrm((DEPTH, NA_DH), 0.02),
        'na_rpb': nrm((DEPTH, NA_HEADS, 2 * NA_WIN_R - 1, 2 * NA_WIN_C - 1), 0.1),
        'w_hb': nrm((DEPTH, HG_WIDTH, d), HG_WIDTH ** -0.5),
        'w_nb': nrm((DEPTH, NA_WIDTH, d), NA_WIDTH ** -0.5),
        'w_out': nrm((DEPTH, d, d), d ** -0.5),
        'ffn_wg': nrm((N_DENSE, d, D_FF), d ** -0.5),
        'ffn_wu': nrm((N_DENSE, d, D_FF), d ** -0.5),
        'ffn_wd': nrm((N_DENSE, D_FF, d), D_FF ** -0.5),
        'moe_router': nrm((N_MOE, d, N_EXPERTS), d ** -0.5),
        'moe_wg': nrm((N_MOE, N_EXPERTS, d, D_FF_E), d ** -0.5),
        'moe_wu': nrm((N_MOE, N_EXPERTS, d, D_FF_E), d ** -0.5),
        'moe_wd': nrm((N_MOE, N_EXPERTS, D_FF_E, d), D_FF_E ** -0.5),
    }


def reference(x_prompt, x_sample, cache_k, cache_v, state_hgrn, c, c_ctx, norm1_g, norm2_g, w_ada, b_ada,
              w_in, hg_lb, hg_onorm_g, na_qn_g, na_kn_g, na_rpb, w_hb, w_nb, w_out, ffn_wg, ffn_wu, ffn_wd,
              moe_router, moe_wg, moe_wu, moe_wd):
    lbs = hgrn_lower_bounds(hg_lb)

    y_p = x_prompt
    zeros = jnp.zeros((x_prompt.shape[0], HG_HEADS, HG_DK, HG_DV), jnp.float32)
    ks_out, vs_out, ss_out = [], [], []
    for l in range(DEPTH):
        sh1, sc1, g1, sh2, sc2, g2 = modulation(c_ctx[None, None, :], w_ada[l], b_ada[l])
        h = rmsnorm(y_p, norm1_g[l]) * (1.0 + sc1) + sh1
        hq, hff, hfb, hi, hg, nq, nk, nv, ga, gb = split_mixer_inputs(h, w_in[l])
        ya, s_ctx = hgrn_branch(hq, hff, hfb, hi, hg, lbs[0, l], lbs[1, l], zeros, zeros, hg_onorm_g[l])
        q, k, v = qk_heads(nq, nk, nv, na_qn_g[l], na_kn_g[l])
        yb = from_heads(context_attention(q, k, v))
        y_p = y_p + g1 * merge_branches(ya, yb, ga, gb, w_hb[l], w_nb[l], w_out[l])
        h2 = rmsnorm(y_p, norm2_g[l]) * (1.0 + sc2) + sh2
        y_p = y_p + g2 * channel_mixer(h2, l, ffn_wg, ffn_wu, ffn_wd, moe_router, moe_wg, moe_wu, moe_wd)
        ks_out.append(k)
        vs_out.append(v)
        ss_out.append(s_ctx)
    new_cache_k = jnp.stack(ks_out, axis=1)
    new_cache_v = jnp.stack(vs_out, axis=1)
    new_state_hgrn = jnp.stack(ss_out, axis=1)

    y_s = x_sample
    for l in range(DEPTH):
        sh1, sc1, g1, sh2, sc2, g2 = modulation(c[:, None, :], w_ada[l], b_ada[l])
        h = rmsnorm(y_s, norm1_g[l]) * (1.0 + sc1) + sh1
        hq, hff, hfb, hi, hg, nq, nk, nv, ga, gb = split_mixer_inputs(h, w_in[l])
        ya, _ = hgrn_branch(hq, hff, hfb, hi, hg, lbs[0, l], lbs[1, l],
                            state_hgrn[:, l, 0], state_hgrn[:, l, 1], hg_onorm_g[l])
        q, k, v = qk_heads(nq, nk, nv, na_qn_g[l], na_kn_g[l])
        yb = from_heads(neighbourhood_attention(q, k, v, cache_k[:, l], cache_v[:, l], na_rpb[l]))
        y_s = y_s + g1 * merge_branches(ya, yb, ga, gb, w_hb[l], w_nb[l], w_out[l])
        h2 = rmsnorm(y_s, norm2_g[l]) * (1.0 + sc2) + sh2
        y_s = y_s + g2 * channel_mixer(h2, l, ffn_wg, ffn_wu, ffn_wd, moe_router, moe_wg, moe_wu, moe_wd)

    y_prompt = y_p
    y_sample = y_s
    return (y_prompt, y_sample, new_cache_k, new_cache_v, new_state_hgrn)
```

```python
import functools

import numpy as np
import jax
import jax.numpy as jnp
from jax import lax
from jax.experimental import pallas as pl
from jax.experimental.pallas import tpu as pltpu

F32 = jnp.float32
BF16 = jnp.bfloat16

NORM_EPS = 1e-6
NEG_INF = -1e30
GRID_W = 64
NA_QB = 16
LANES = 128
N_COND = 16
HG_BLOCK = 32
HG_HALF = HG_BLOCK // 2
VMEM_LIMIT = 56 * 1024 * 1024


def _cparams(semantics, vmem=VMEM_LIMIT):
    return pltpu.CompilerParams(dimension_semantics=semantics, vmem_limit_bytes=vmem)


def _silu(x):
    return x * jax.nn.sigmoid(x)


def _bdot(a, b):
    return jnp.dot(a.astype(BF16), b.astype(BF16), preferred_element_type=F32)


def _bdot_nt(a, b):
    return lax.dot_general(a.astype(BF16), b.astype(BF16), (((1,), (1,)), ((), ())),
                           preferred_element_type=F32)


def _bdot_tn(a, b):
    return lax.dot_general(a.astype(BF16), b.astype(BF16), (((0,), (0,)), ((), ())),
                           preferred_element_type=F32)


def _pick(n, pref):
    if n <= pref:
        return n
    t = pref - pref % LANES
    while n % t:
        t -= LANES
    return t


def _cond_of_rows(row0, n_prompt, ts):
    return jnp.where(row0 < n_prompt, 0, 1 + (row0 - n_prompt) // ts)


def _mod_body(c_ref, w_ref, b_ref, o_ref):
    o_ref[0] = _bdot(_silu(c_ref[...]), w_ref[0]) + b_ref[0]


def _modulation(cond, w_ada, b_ada):
    depth, d, n6 = w_ada.shape
    tn = _pick(n6, 1024)
    return pl.pallas_call(
        _mod_body,
        grid=(depth, n6 // tn),
        in_specs=[
            pl.BlockSpec((N_COND, d), lambda l, j: (0, 0)),
            pl.BlockSpec((1, d, tn), lambda l, j: (l, 0, j)),
            pl.BlockSpec((1, 1, tn), lambda l, j: (l, 0, j)),
        ],
        out_specs=pl.BlockSpec((1, N_COND, tn), lambda l, j: (l, 0, j)),
        out_shape=jax.ShapeDtypeStruct((depth, N_COND, n6), F32),
        compiler_params=_cparams(("parallel", "parallel")),
        name="modulation",
    )(cond, w_ada, b_ada.reshape(depth, 1, n6))


def _rms_modulate(x, g, shift, scale):
    ms = jnp.mean(x * x, axis=-1, keepdims=True)
    return (x * lax.rsqrt(ms + NORM_EPS) * g) * (1.0 + scale) + shift


def _prenorm_body(y_ref, g_ref, m_ref, o_ref):
    o_ref[...] = _rms_modulate(y_ref[...], g_ref[...], m_ref[0, 0:1, :], m_ref[0, 1:2, :]).astype(o_ref.dtype)


def _prenorm(y, g, mod_l, n_prompt, ts):
    n, d = y.shape
    tm = min(512, ts)
    return pl.pallas_call(
        _prenorm_body,
        grid=(n // tm,),
        in_specs=[
            pl.BlockSpec((tm, d), lambda j: (j, 0)),
            pl.BlockSpec((1, d), lambda j: (0, 0)),
            pl.BlockSpec((1, 6, d), lambda j: (_cond_of_rows(j * tm, n_prompt, ts), 0, 0)),
        ],
        out_specs=pl.BlockSpec((tm, d), lambda j: (j, 0)),
        out_shape=jax.ShapeDtypeStruct((n, d), BF16),
        compiler_params=_cparams(("parallel",)),
        name="prenorm",
    )(y, g.reshape(1, d), mod_l)


def _mm_body(x_ref, w_ref, o_ref):
    o_ref[...] = jnp.dot(x_ref[...], w_ref[...], preferred_element_type=F32).astype(o_ref.dtype)


def _matmul(x, w, out_dtype=F32):
    m, k = x.shape
    n = w.shape[1]
    tm, tn = _pick(m, 1024), _pick(n, 1024)
    return pl.pallas_call(
        _mm_body,
        grid=(m // tm, n // tn),
        in_specs=[pl.BlockSpec((tm, k), lambda i, j: (i, 0)), pl.BlockSpec((k, tn), lambda i, j: (0, j))],
        out_specs=pl.BlockSpec((tm, tn), lambda i, j: (i, j)),
        out_shape=jax.ShapeDtypeStruct((m, n), out_dtype),
        compiler_params=_cparams(("parallel", "parallel")),
        name="in_proj",
    )(x, w)


def _hgrn_gates(z, lb):
    t = jnp.exp(-jnp.abs(z))
    pos = z >= 0.0
    log_f = jnp.where(pos, jnp.log1p(lb * t), jnp.log(lb + t)) - jnp.log1p(t)
    k = (1.0 - lb) * jnp.where(pos, t, 1.0) / (1.0 + t)
    return log_f, k


def _split3(x):
    hi = x.astype(BF16)
    r = x - hi.astype(F32)
    mid = r.astype(BF16)
    lo = (r - mid.astype(F32)).astype(BF16)
    return hi, mid, lo


def _hgrn_body(*refs, seq, hp, dh, has_s0, has_sfin):
    zq, zff, zfb, zi, zg, lb_ref, og_ref = refs[:7]
    pos = 7
    s0_ref = None
    if has_s0:
        s0_ref = refs[pos]
        pos += 1
    ya_ref = refs[pos]
    pos += 1
    sfin_ref = None
    if has_sfin:
        sfin_ref = refs[pos]
        pos += 1
    q_s, lf_s, k_s, o_s, st_s = refs[pos:pos + 5]
    w = hp * dh
    nblk = seq // HG_BLOCK

    q_s[...] = _silu(zq[...])
    for d, zf in enumerate((zff, zfb)):
        lf, k = _hgrn_gates(zf[...], lb_ref[d:d + 1, :])
        lf_s[d] = lf
        k_s[d] = k
    for d in range(2):
        for h in range(hp):
            if has_s0:
                st_s[d * hp + h] = s0_ref[0, 0, d, h].T
            else:
                st_s[d * hp + h] = jnp.zeros((dh, dh), F32)

    row = lax.broadcasted_iota(jnp.int32, (HG_BLOCK, HG_BLOCK), 0)
    col = lax.broadcasted_iota(jnp.int32, (HG_BLOCK, HG_BLOCK), 1)
    same_half = (row // HG_HALF) == (col // HG_HALF)
    rowv = lax.broadcasted_iota(jnp.int32, (HG_BLOCK, 1), 0)
    attend = (col <= row, col >= row)
    seg = tuple(jnp.where(a & same_half, 1.0, 0.0).astype(BF16) for a in attend)
    first_half = (rowv < HG_HALF, rowv >= HG_HALF)
    first_end = (HG_HALF - 1, HG_HALF)
    second_end = (HG_BLOCK - 1, 0)

    def block(i, carry):
        for d in range(2):
            r0 = i * HG_BLOCK if d == 0 else seq - HG_BLOCK * (i + 1)
            rows = pl.ds(pl.multiple_of(r0, HG_BLOCK), HG_BLOCK)
            qb = q_s[rows, :]
            kb = k_s[d, rows, :]
            vb = zi[rows, :]
            hi, mid, lo = _split3(lf_s[d, rows, :])
            cum3 = jnp.dot(seg[d], jnp.concatenate([hi, mid, lo], axis=1), preferred_element_type=F32)
            cum = cum3[:, :w] + cum3[:, w:2 * w] + cum3[:, 2 * w:]
            l_first = cum[first_end[d]:first_end[d] + 1, :]
            l_second = cum[second_end[d]:second_end[d] + 1, :]
            l_blk = l_first + l_second
            cum_blk = cum + jnp.where(first_half[d], 0.0, l_first)
            e = cum_blk - l_first
            qe = qb * jnp.exp(e)
            ke = kb * jnp.exp(-e)
            qs = qb * jnp.exp(cum_blk)
            kend = kb * jnp.exp(l_blk - cum_blk)
            dec = jnp.exp(l_blk)
            for h in range(hp):
                sl = slice(h * dh, (h + 1) * dh)
                a = jnp.where(attend[d], _bdot_nt(qe[:, sl], ke[:, sl]), 0.0)
                st = st_s[d * hp + h]
                o_s[d, rows, sl] = _bdot(a, vb[:, sl]) + _bdot_nt(qs[:, sl], st)
                st_s[d * hp + h] = st * dec[:, sl] + _bdot_tn(vb[:, sl], kend[:, sl])
        return carry

    lax.fori_loop(0, nblk, block, 0)

    o = o_s[0] + o_s[1]
    gate = _silu(zg[...])
    for h in range(hp):
        sl = slice(h * dh, (h + 1) * dh)
        oh = o[:, sl]
        ms = jnp.mean(oh * oh, axis=-1, keepdims=True)
        ya_ref[:, sl] = (oh * lax.rsqrt(ms + NORM_EPS) * og_ref[...] * gate[:, sl]).astype(ya_ref.dtype)
    if has_sfin:
        for d in range(2):
            for h in range(hp):
                sfin_ref[0, d, h] = st_s[d * hp + h].T


def _hgrn(z, lb_l, og_l, *, row_blk0, nb, seq, heads, dh, layer, s0=None, want_state=False):
    hp = 2 if heads % 2 == 0 else 1
    w = hp * dh
    hw = heads * dh
    ng = heads // hp
    cb = hw // w

    def zspec(k):
        return pl.BlockSpec((seq, w), lambda b, g, k=k: (row_blk0 + b, k * cb + g))

    in_specs = [zspec(0), zspec(1), zspec(2), zspec(3), zspec(4),
                pl.BlockSpec((2, w), lambda b, g: (0, g)),
                pl.BlockSpec((1, dh), lambda b, g: (0, 0))]
    args = [z, z, z, z, z, lb_l, og_l.reshape(1, dh)]
    if s0 is not None:
        in_specs.append(pl.BlockSpec((1, 1, 2, hp, dh, dh), lambda b, g: (b, layer, 0, g, 0, 0)))
        args.append(s0)
    out_specs = [pl.BlockSpec((seq, w), lambda b, g: (b, g))]
    out_shape = [jax.ShapeDtypeStruct((nb * seq, hw), BF16)]
    if want_state:
        out_specs.append(pl.BlockSpec((1, 2, hp, dh, dh), lambda b, g: (b, 0, g, 0, 0)))
        out_shape.append(jax.ShapeDtypeStruct((nb, 2, heads, dh, dh), F32))
    res = pl.pallas_call(
        functools.partial(_hgrn_body, seq=seq, hp=hp, dh=dh, has_s0=s0 is not None, has_sfin=want_state),
        grid=(nb, ng),
        in_specs=in_specs,
        out_specs=out_specs,
        out_shape=out_shape,
        scratch_shapes=[
            pltpu.VMEM((seq, w), F32),
            pltpu.VMEM((2, seq, w), F32),
            pltpu.VMEM((2, seq, w), F32),
            pltpu.VMEM((2, seq, w), F32),
            pltpu.VMEM((2 * hp, dh, dh), F32),
        ],
        compiler_params=_cparams(("parallel", "parallel")),
        name="hgrn",
    )(*args)
    return res if want_state else res[0]


def _head_rms(x, g):
    ms = jnp.mean(x * x, axis=-1, keepdims=True)
    return x * lax.rsqrt(ms + NORM_EPS) * g


def _ctx_attn_body(zq, zk, zv, qg_ref, kg_ref, yb_ref, k_ref, v_ref, *, hp, dh):
    scale = dh ** -0.5
    for h in range(hp):
        sl = slice(h * dh, (h + 1) * dh)
        q = _head_rms(zq[:, sl], qg_ref[...])
        k = _head_rms(zk[:, sl], kg_ref[...])
        v = zv[:, sl]
        k_ref[0, 0, h] = k
        v_ref[0, 0, h] = v
        s = _bdot_nt(q, k) * scale
        p = jnp.exp(s - jnp.max(s, axis=-1, keepdims=True))
        o = _bdot(p, v) / jnp.sum(p, axis=-1, keepdims=True)
        yb_ref[:, sl] = o.astype(yb_ref.dtype)


def _ctx_attention(z, qg, kg, *, nb, seq, heads, dh, col0):
    hp = 2 if heads % 2 == 0 else 1
    w = hp * dh
    nw = heads * dh
    ng = heads // hp
    c0 = col0 // w
    cb = nw // w

    def zspec(k):
        return pl.BlockSpec((seq, w), lambda b, g, k=k: (b, c0 + k * cb + g))

    kv_spec = pl.BlockSpec((1, 1, hp, seq, dh), lambda b, g: (b, 0, g, 0, 0))
    kv_shape = jax.ShapeDtypeStruct((nb, 1, heads, seq, dh), F32)
    return pl.pallas_call(
        functools.partial(_ctx_attn_body, hp=hp, dh=dh),
        grid=(nb, ng),
        in_specs=[zspec(0), zspec(1), zspec(2),
                  pl.BlockSpec((1, dh), lambda b, g: (0, 0)), pl.BlockSpec((1, dh), lambda b, g: (0, 0))],
        out_specs=[pl.BlockSpec((seq, w), lambda b, g: (b, g)), kv_spec, kv_spec],
        out_shape=[jax.ShapeDtypeStruct((nb * seq, nw), BF16), kv_shape, kv_shape],
        compiler_params=_cparams(("parallel", "parallel")),
        name="ctx_attention",
    )(z, z, z, qg.reshape(1, dh), kg.reshape(1, dh))


def _na_bias_index(seq, win_r, win_c):
    rows = seq // GRID_W
    wr = min(win_r, rows)
    t = np.arange(seq)
    r, c = t // GRID_W, t % GRID_W
    r0 = np.clip(r - wr // 2, 0, rows - wr)
    c0 = np.clip(c - win_c // 2, 0, GRID_W - win_c)
    kr, kc = r[None, :], c[None, :]
    valid = ((kr >= r0[:, None]) & (kr < r0[:, None] + wr) & (kc >= c0[:, None]) & (kc < c0[:, None] + win_c))
    ridx = np.clip(kr - r[:, None] + win_r - 1, 0, 2 * win_r - 2)
    cidx = np.clip(kc - c[:, None] + win_c - 1, 0, 2 * win_c - 2)
    return valid, ridx, cidx


def _na_attn_body(zq, zk, zv, kc_ref, vc_ref, bias_ref, qg_ref, kg_ref, yb_ref, *, seq, dh, tq):
    scale = dh ** -0.5
    q = _head_rms(zq[...], qg_ref[...]).astype(BF16)
    k = _head_rms(zk[...], kg_ref[...]).astype(BF16)
    v = zv[...].astype(BF16)
    kc = kc_ref[0, 0, 0].astype(BF16)
    vc = vc_ref[0, 0, 0].astype(BF16)
    for i in range(seq // tq):
        rows = slice(i * tq, (i + 1) * tq)
        qt = q[rows]
        s_win = _bdot_nt(qt, k) * scale + bias_ref[0, rows, :]
        s_ctx = _bdot_nt(qt, kc) * scale
        m = jnp.maximum(jnp.max(s_win, axis=-1, keepdims=True), jnp.max(s_ctx, axis=-1, keepdims=True))
        p_win = jnp.exp(s_win - m)
        p_ctx = jnp.exp(s_ctx - m)
        den = jnp.sum(p_win, axis=-1, keepdims=True) + jnp.sum(p_ctx, axis=-1, keepdims=True)
        o = (_bdot(p_win, v) + _bdot(p_ctx, vc)) / den
        yb_ref[rows, :] = o.astype(yb_ref.dtype)


def _na_attention(z, cache_k, cache_v, bias, qg, kg, *, row_blk0, nb, seq, heads, dh, col0, layer):
    nw = heads * dh
    c0 = col0 // dh
    past = cache_k.shape[3]

    def zspec(k):
        return pl.BlockSpec((seq, dh), lambda h, b, k=k: (row_blk0 + b, c0 + k * heads + h))

    cache_spec = pl.BlockSpec((1, 1, 1, past, dh), lambda h, b: (b, layer, h, 0, 0))
    return pl.pallas_call(
        functools.partial(_na_attn_body, seq=seq, dh=dh, tq=min(256, seq)),
        grid=(heads, nb),
        in_specs=[zspec(0), zspec(1), zspec(2), cache_spec, cache_spec,
                  pl.BlockSpec((1, seq, seq), lambda h, b: (h, 0, 0)),
                  pl.BlockSpec((1, dh), lambda h, b: (0, 0)), pl.BlockSpec((1, dh), lambda h, b: (0, 0))],
        out_specs=pl.BlockSpec((seq, dh), lambda h, b: (b, h)),
        out_shape=jax.ShapeDtypeStruct((nb * seq, nw), BF16),
        compiler_params=_cparams(("parallel", "parallel")),
        name="na_attention",
    )(z, z, z, cache_k, cache_v, bias, qg.reshape(1, dh), kg.reshape(1, dh))


def _merge_body(ya_ref, yb_ref, wh_ref, wn_ref, ga_ref, gb_ref, o_ref):
    a = jnp.dot(ya_ref[...], wh_ref[...], preferred_element_type=F32)
    b = jnp.dot(yb_ref[...], wn_ref[...], preferred_element_type=F32)
    o_ref[...] = (jax.nn.sigmoid(ga_ref[...]) * a + jax.nn.sigmoid(gb_ref[...]) * b).astype(o_ref.dtype)


def _merge(ya, yb, w_hb, w_nb, z, col_ga):
    n, hw = ya.shape
    nw = yb.shape[1]
    d = w_hb.shape[1]
    tm, tn = _pick(n, 512), _pick(d, 1024)
    ca, cbk = col_ga // tn, (col_ga + d) // tn
    return pl.pallas_call(
        _merge_body,
        grid=(n // tm, d // tn),
        in_specs=[
            pl.BlockSpec((tm, hw), lambda i, j: (i, 0)),
            pl.BlockSpec((tm, nw), lambda i, j: (i, 0)),
            pl.BlockSpec((hw, tn), lambda i, j: (0, j)),
            pl.BlockSpec((nw, tn), lambda i, j: (0, j)),
            pl.BlockSpec((tm, tn), lambda i, j: (i, ca + j)),
            pl.BlockSpec((tm, tn), lambda i, j: (i, cbk + j)),
        ],
        out_specs=pl.BlockSpec((tm, tn), lambda i, j: (i, j)),
        out_shape=jax.ShapeDtypeStruct((n, d), BF16),
        compiler_params=_cparams(("parallel", "parallel")),
        name="merge",
    )(ya, yb, w_hb, w_nb, z, z)


def _outproj_body(*refs, n_experts):
    m_ref, w_ref, y_ref, mod_ref, g_ref = refs[:5]
    if n_experts:
        r_ref, y1_ref, h2_ref, gw_ref, gi_ref = refs[5:]
    else:
        y1_ref, h2_ref = refs[5:]
    y1 = y_ref[...] + mod_ref[0, 2:3, :] * jnp.dot(m_ref[...], w_ref[...], preferred_element_type=F32)
    y1_ref[...] = y1
    h2 = _rms_modulate(y1, g_ref[...], mod_ref[0, 3:4, :], mod_ref[0, 4:5, :])
    h2_ref[...] = h2.astype(h2_ref.dtype)
    if n_experts:
        r = r_ref[...]
        h_hi = h2.astype(BF16)
        h_lo = (h2 - h_hi.astype(F32)).astype(BF16)
        r_hi = r.astype(BF16)
        r_lo = (r - r_hi.astype(F32)).astype(BF16)
        logits = (jnp.dot(h_hi, r_hi, preferred_element_type=F32)
                  + jnp.dot(h_hi, r_lo, preferred_element_type=F32)
                  + jnp.dot(h_lo, r_hi, preferred_element_type=F32))
        lane = lax.broadcasted_iota(jnp.int32, logits.shape, 1).astype(F32)
        big = float(LANES)
        lg = jnp.where(lane < n_experts, logits, -jnp.inf)
        m1 = jnp.max(lg, axis=-1, keepdims=True)
        i1 = jnp.min(jnp.where(lg == m1, lane, big), axis=-1, keepdims=True)
        lg2 = jnp.where(lane == i1, -jnp.inf, lg)
        m2 = jnp.max(lg2, axis=-1, keepdims=True)
        i2 = jnp.min(jnp.where(lg2 == m2, lane, big), axis=-1, keepdims=True)
        t = jnp.exp(m2 - m1)
        w1 = 1.0 / (1.0 + t)
        w2 = t / (1.0 + t)
        gw_ref[...] = jnp.where(lane == 0.0, w1, jnp.where(lane == 1.0, w2, 0.0))
        gi_ref[...] = jnp.where(lane == 0.0, i1, jnp.where(lane == 1.0, i2, 0.0)).astype(jnp.int32)


def _outproj(m, w_out, y, mod_l, g2, n_prompt, ts, router=None):
    n, d = y.shape
    tm = min(256, ts)
    n_experts = 0 if router is None else router.shape[1]
    in_specs = [
        pl.BlockSpec((tm, d), lambda j: (j, 0)),
        pl.BlockSpec((d, d), lambda j: (0, 0)),
        pl.BlockSpec((tm, d), lambda j: (j, 0)),
        pl.BlockSpec((1, 6, d), lambda j: (_cond_of_rows(j * tm, n_prompt, ts), 0, 0)),
        pl.BlockSpec((1, d), lambda j: (0, 0)),
    ]
    args = [m, w_out, y, mod_l, g2.reshape(1, d)]
    row_spec = pl.BlockSpec((tm, d), lambda j: (j, 0))
    out_specs = [row_spec, row_spec]
    out_shape = [jax.ShapeDtypeStruct((n, d), F32), jax.ShapeDtypeStruct((n, d), F32 if n_experts else BF16)]
    if n_experts:
        in_specs.append(pl.BlockSpec((d, LANES), lambda j: (0, 0)))
        args.append(jnp.pad(router, ((0, 0), (0, LANES - n_experts))))
        lane_spec = pl.BlockSpec((tm, LANES), lambda j: (j, 0))
        out_specs += [lane_spec, lane_spec]
        out_shape += [jax.ShapeDtypeStruct((n, LANES), F32), jax.ShapeDtypeStruct((n, LANES), jnp.int32)]
    return pl.pallas_call(
        functools.partial(_outproj_body, n_experts=n_experts),
        grid=(n // tm,),
        in_specs=in_specs,
        out_specs=out_specs,
        out_shape=out_shape,
        compiler_params=_cparams(("parallel",)),
        name="outproj",
    )(*args)


def _ffn_body(te_ref, nt_ref, *refs, residual):
    del te_ref
    if residual:
        x_ref, wg_ref, wu_ref, wd_ref, y_ref, mod_ref, o_ref = refs
    else:
        x_ref, wg_ref, wu_ref, wd_ref, o_ref = refs
    j, f = pl.program_id(0), pl.program_id(1)
    nf = pl.num_programs(1)

    @pl.when(j < nt_ref[0])
    def _():
        x = x_ref[...].astype(BF16)
        g = jnp.dot(x, wg_ref[0], preferred_element_type=F32)
        u = jnp.dot(x, wu_ref[0], preferred_element_type=F32)
        part = jnp.dot((_silu(g) * u).astype(BF16), wd_ref[0], preferred_element_type=F32)

        @pl.when(f == 0)
        def _():
            o_ref[...] = part

        @pl.when(f > 0)
        def _():
            o_ref[...] += part

        if residual:
            @pl.when(f == nf - 1)
            def _():
                o_ref[...] = y_ref[...] + mod_ref[0, 5:6, :] * o_ref[...]

    @pl.when((j >= nt_ref[0]) & (f == 0))
    def _():
        o_ref[...] = jnp.zeros_like(o_ref)


def _ffn(x, wg, wu, wd, tile_expert, n_tiles, *, tm, residual=None):
    n, d = x.shape
    ff = wg.shape[2]
    tf = _pick(ff, 512)
    nf = ff // tf

    def row_map(j, f, te, nt):
        return (jnp.minimum(j, nt[0] - 1), 0)

    def ff_idx(j, f, nt):
        return jnp.where(j < nt[0], f, nf - 1)

    def e_idx(j, te, nt):
        return te[jnp.minimum(j, nt[0] - 1)]

    in_specs = [
        pl.BlockSpec((tm, d), row_map),
        pl.BlockSpec((1, d, tf), lambda j, f, te, nt: (e_idx(j, te, nt), 0, ff_idx(j, f, nt))),
        pl.BlockSpec((1, d, tf), lambda j, f, te, nt: (e_idx(j, te, nt), 0, ff_idx(j, f, nt))),
        pl.BlockSpec((1, tf, d), lambda j, f, te, nt: (e_idx(j, te, nt), ff_idx(j, f, nt), 0)),
    ]
    args = [x, wg, wu, wd]
    if residual is not None:
        y, mod_l, n_prompt, ts = residual
        in_specs += [
            pl.BlockSpec((tm, d), row_map),
            pl.BlockSpec((1, 6, d), lambda j, f, te, nt: (_cond_of_rows(j * tm, n_prompt, ts), 0, 0)),
        ]
        args += [y, mod_l]
    return pl.pallas_call(
        functools.partial(_ffn_body, residual=residual is not None),
        grid_spec=pltpu.PrefetchScalarGridSpec(
            num_scalar_prefetch=2,
            grid=(n // tm, nf),
            in_specs=in_specs,
            out_specs=pl.BlockSpec((tm, d), lambda j, f, te, nt: (j, 0)),
        ),
        out_shape=jax.ShapeDtypeStruct((n, d), F32),
        compiler_params=_cparams(("arbitrary", "arbitrary")),
        name="swiglu",
    )(tile_expert, n_tiles, *args)


def _dispatch_body(p0_ref, p1_ref, src, dst_in, dst, sem, *, tb):
    del dst_in
    base = pl.program_id(0) * tb

    def issue(t, c):
        row = src.at[pl.ds(base + t, 1)]
        pltpu.make_async_copy(row, dst.at[pl.ds(p0_ref[0, 0, t], 1)], sem).start()
        pltpu.make_async_copy(row, dst.at[pl.ds(p1_ref[0, 0, t], 1)], sem).start()
        return c

    lax.fori_loop(0, tb, issue, 0)
    for _ in range(2):
        pltpu.make_async_copy(src.at[pl.ds(0, tb)], dst.at[pl.ds(0, tb)], sem).wait()


def _dispatch(h, pos0, pos1, n_sorted, tb):
    n, d = h.shape
    return pl.pallas_call(
        functools.partial(_dispatch_body, tb=tb),
        grid=(n // tb,),
        in_specs=[
            pl.BlockSpec((1, 1, tb), lambda i: (i, 0, 0), memory_space=pltpu.SMEM),
            pl.BlockSpec((1, 1, tb), lambda i: (i, 0, 0), memory_space=pltpu.SMEM),
            pl.BlockSpec(memory_space=pl.ANY),
            pl.BlockSpec(memory_space=pl.ANY),
        ],
        out_specs=pl.BlockSpec(memory_space=pl.ANY),
        out_shape=jax.ShapeDtypeStruct((n_sorted, d), h.dtype),
        scratch_shapes=[pltpu.SemaphoreType.DMA(())],
        input_output_aliases={3: 0},
        compiler_params=_cparams(("arbitrary",)),
        name="dispatch",
    )(pos0.reshape(n // tb, 1, tb), pos1.reshape(n // tb, 1, tb), h, jnp.zeros((n_sorted, d), h.dtype))


def _combine_body(p0_ref, p1_ref, o_hbm, y_ref, gw_ref, mod_ref, out_ref, buf0, buf1, sem, *, tc):
    def issue(t, c):
        pltpu.make_async_copy(o_hbm.at[pl.ds(p0_ref[0, 0, t], 1)], buf0.at[pl.ds(t, 1)], sem).start()
        pltpu.make_async_copy(o_hbm.at[pl.ds(p1_ref[0, 0, t], 1)], buf1.at[pl.ds(t, 1)], sem).start()
        return c

    lax.fori_loop(0, tc, issue, 0)
    pltpu.make_async_copy(o_hbm.at[pl.ds(0, tc)], buf0, sem).wait()
    pltpu.make_async_copy(o_hbm.at[pl.ds(0, tc)], buf1, sem).wait()
    gw = gw_ref[...]
    mix = gw[:, 0:1] * buf0[...] + gw[:, 1:2] * buf1[...]
    out_ref[...] = y_ref[...] + mod_ref[0, 5:6, :] * mix


def _combine(o_sorted, pos0, pos1, y, gw, mod_l, n_prompt, ts):
    n, d = y.shape
    tc = min(256, ts)
    pos0, pos1 = pos0.reshape(n // tc, 1, tc), pos1.reshape(n // tc, 1, tc)
    return pl.pallas_call(
        functools.partial(_combine_body, tc=tc),
        grid=(n // tc,),
        in_specs=[
            pl.BlockSpec((1, 1, tc), lambda i: (i, 0, 0), memory_space=pltpu.SMEM),
            pl.BlockSpec((1, 1, tc), lambda i: (i, 0, 0), memory_space=pltpu.SMEM),
            pl.BlockSpec(memory_space=pl.ANY),
            pl.BlockSpec((tc, d), lambda i: (i, 0)),
            pl.BlockSpec((tc, LANES), lambda i: (i, 0)),
            pl.BlockSpec((1, 6, d), lambda i: (_cond_of_rows(i * tc, n_prompt, ts), 0, 0)),
        ],
        out_specs=pl.BlockSpec((tc, d), lambda i: (i, 0)),
        out_shape=jax.ShapeDtypeStruct((n, d), F32),
        scratch_shapes=[pltpu.VMEM((tc, d), F32), pltpu.VMEM((tc, d), F32), pltpu.SemaphoreType.DMA(())],
        compiler_params=_cparams(("arbitrary",)),
        name="combine",
    )(pos0, pos1, o_sorted, y, gw, mod_l)


def _routing_tables(top_i, n_experts, tm, n_tiles_max):
    n = top_i.shape[0]
    flat = jnp.concatenate([top_i[:, 0], top_i[:, 1]])
    onehot = (flat[:, None] == jnp.arange(n_experts, dtype=jnp.int32)[None, :]).astype(jnp.int32)
    csum = jnp.cumsum(onehot, axis=0)
    rank = jnp.sum((csum - onehot) * onehot, axis=1)
    counts = csum[-1]
    tiles = (counts + tm - 1) // tm
    tile_end = jnp.cumsum(tiles)
    start = (tile_end - tiles) * tm
    pos = jnp.sum(onehot * start[None, :], axis=1) + rank
    n_tiles = tile_end[-1]
    j = jnp.arange(n_tiles_max, dtype=jnp.int32)
    tile_expert = jnp.sum((j[:, None] >= tile_end[None, :]).astype(jnp.int32), axis=1)
    tile_expert = jnp.minimum(tile_expert, n_experts - 1).astype(jnp.int32)
    return pos[:n].astype(jnp.int32), pos[n:].astype(jnp.int32), tile_expert, n_tiles.reshape(1).astype(jnp.int32)


def kernel(x_prompt, x_sample, cache_k, cache_v, state_hgrn, c, c_ctx, norm1_g, norm2_g, w_ada, b_ada, w_in,
           hg_lb, hg_onorm_g, na_qn_g, na_kn_g, na_rpb, w_hb, w_nb, w_out, ffn_wg, ffn_wu, ffn_wd,
           moe_router, moe_wg, moe_wu, moe_wd):
    bp, tp, d = x_prompt.shape
    bs, ts, _ = x_sample.shape
    depth = w_in.shape[0]
    hg_heads, dh = state_hgrn.shape[3], state_hgrn.shape[4]
    na_heads = cache_k.shape[2]
    hw, nw = hg_heads * dh, na_heads * dh
    n_experts = moe_wg.shape[1]
    win_r, win_c = (na_rpb.shape[2] + 1) // 2, (na_rpb.shape[3] + 1) // 2
    n_prompt, n_sample = bp * tp, bs * ts
    n = n_prompt + n_sample
    assert n_prompt % ts == 0 and ts % tp == 0 and bs + 1 <= N_COND
    col_nq = 5 * hw
    col_ga = 5 * hw + 3 * nw

    p = jax.nn.softmax(hg_lb.astype(F32), axis=1)
    cs = jnp.cumsum(p, axis=1)
    lbs = cs - cs[:, :1]
    valid, ridx, cidx = _na_bias_index(ts, win_r, win_c)
    bias = jnp.where(valid[None, None], na_rpb.astype(F32)[:, :, ridx, cidx], NEG_INF)
    w_in_b, w_hb_b, w_nb_b, w_out_b = (a.astype(BF16) for a in (w_in, w_hb, w_nb, w_out))
    ffn_b = tuple(a.astype(BF16) for a in (ffn_wg, ffn_wu, ffn_wd))
    moe_b = tuple(a.astype(BF16) for a in (moe_wg, moe_wu, moe_wd))

    cond = jnp.concatenate([c_ctx[None, :], c, jnp.zeros((N_COND - 1 - bs, d), F32)], axis=0)
    mod = _modulation(cond, w_ada, b_ada).reshape(depth, N_COND, 6, d)

    y = jnp.concatenate([x_prompt.reshape(n_prompt, d), x_sample.reshape(n_sample, d)], axis=0)
    tm_moe = min(512, n)
    n_tiles_max = (2 * n) // tm_moe + n_experts
    ks_out, vs_out, ss_out = [], [], []
    for l in range(depth):
        h = _prenorm(y, norm1_g[l], mod[l], n_prompt, ts)
        z = _matmul(h, w_in_b[l])
        ya_p, s_ctx = _hgrn(z, lbs[:, l], hg_onorm_g[l], row_blk0=0, nb=bp, seq=tp, heads=hg_heads, dh=dh,
                            layer=l, want_state=True)
        ya_s = _hgrn(z, lbs[:, l], hg_onorm_g[l], row_blk0=n_prompt // ts, nb=bs, seq=ts, heads=hg_heads,
                     dh=dh, layer=l, s0=state_hgrn)
        yb_p, k_new, v_new = _ctx_attention(z, na_qn_g[l], na_kn_g[l], nb=bp, seq=tp, heads=na_heads, dh=dh,
                                            col0=col_nq)
        yb_s = _na_attention(z, cache_k, cache_v, bias[l], na_qn_g[l], na_kn_g[l], row_blk0=n_prompt // ts,
                             nb=bs, seq=ts, heads=na_heads, dh=dh, col0=col_nq, layer=l)
        ya = jnp.concatenate([ya_p, ya_s], axis=0)
        yb = jnp.concatenate([yb_p, yb_s], axis=0)
        m = _merge(ya, yb, w_hb_b[l], w_nb_b[l], z, col_ga)
        i = l // 2
        if l % 2 == 0:
            y1, h2 = _outproj(m, w_out_b[l], y, mod[l], norm2_g[l], n_prompt, ts)
            tm = min(512, ts)
            y = _ffn(h2, ffn_b[0][i][None], ffn_b[1][i][None], ffn_b[2][i][None],
                     jnp.zeros((n // tm,), jnp.int32), jnp.full((1,), n // tm, jnp.int32), tm=tm,
                     residual=(y1, mod[l], n_prompt, ts))
        else:
            y1, h2, gw, gi = _outproj(m, w_out_b[l], y, mod[l], norm2_g[l], n_prompt, ts, router=moe_router[i])
            pos0, pos1, tile_expert, n_tiles = _routing_tables(gi[:, :2], n_experts, tm_moe, n_tiles_max)
            xs = _dispatch(h2, pos0, pos1, n_tiles_max * tm_moe, min(1024, ts))
            o_sorted = _ffn(xs, moe_b[0][i], moe_b[1][i], moe_b[2][i], tile_expert, n_tiles, tm=tm_moe)
            y = _combine(o_sorted, pos0, pos1, y1, gw, mod[l], n_prompt, ts)
        ks_out.append(k_new)
        vs_out.append(v_new)
        ss_out.append(s_ctx)

    y_prompt = y[:n_prompt].reshape(bp, tp, d)
    y_sample = y[n_prompt:].reshape(bs, ts, d)
    new_cache_k = jnp.concatenate(ks_out, axis=1)
    new_cache_v = jnp.concatenate(vs_out, axis=1)
    new_state_hgrn = jnp.stack(ss_out, axis=1)
    return (y_prompt, y_sample, new_cache_k, new_cache_v, new_state_hgrn)
```

```python
import functools

import numpy as np
import jax
import jax.numpy as jnp
from jax import lax
from jax.experimental import pallas as pl
from jax.experimental.pallas import tpu as pltpu

F32 = jnp.float32
BF16 = jnp.bfloat16

NORM_EPS = 1e-6
NEG_INF = -1e30
GRID_W = 64
NA_QB = 16
LANES = 128
N_COND = 16
HG_BLOCK = 32
HG_HALF = HG_BLOCK // 2
VMEM_LIMIT = 56 * 1024 * 1024


def _cparams(semantics, vmem=VMEM_LIMIT):
    return pltpu.CompilerParams(dimension_semantics=semantics, vmem_limit_bytes=vmem)


def _silu(x):
    return x * jax.nn.sigmoid(x)


def _bdot(a, b):
    return jnp.dot(a.astype(BF16), b.astype(BF16), preferred_element_type=F32)


def _bdot_nt(a, b):
    return lax.dot_general(a.astype(BF16), b.astype(BF16), (((1,), (1,)), ((), ())),
                           preferred_element_type=F32)


def _bdot_tn(a, b):
    return lax.dot_general(a.astype(BF16), b.astype(BF16), (((0,), (0,)), ((), ())),
                           preferred_element_type=F32)


def _pick(n, pref):
    if n <= pref:
        return n
    t = pref - pref % LANES
    while n % t:
        t -= LANES
    return t


def _cond_of_rows(row0, n_prompt, ts):
    return jnp.where(row0 < n_prompt, 0, 1 + (row0 - n_prompt) // ts)


def _mod_body(c_ref, w_ref, b_ref, o_ref):
    o_ref[0] = _bdot(_silu(c_ref[...]), w_ref[0]) + b_ref[0]


def _modulation(cond, w_ada, b_ada):
    depth, d, n6 = w_ada.shape
    tn = _pick(n6, 1024)
    return pl.pallas_call(
        _mod_body,
        grid=(depth, n6 // tn),
        in_specs=[
            pl.BlockSpec((N_COND, d), lambda l, j: (0, 0)),
            pl.BlockSpec((1, d, tn), lambda l, j: (l, 0, j)),
            pl.BlockSpec((1, 1, tn), lambda l, j: (l, 0, j)),
        ],
        out_specs=pl.BlockSpec((1, N_COND, tn), lambda l, j: (l, 0, j)),
        out_shape=jax.ShapeDtypeStruct((depth, N_COND, n6), F32),
        compiler_params=_cparams(("parallel", "parallel")),
        name="modulation",
    )(cond, w_ada, b_ada.reshape(depth, 1, n6))


def _rms_modulate(x, g, shift, scale):
    ms = jnp.mean(x * x, axis=-1, keepdims=True)
    return (x * lax.rsqrt(ms + NORM_EPS) * g) * (1.0 + scale) + shift


def _prenorm_body(y_ref, g_ref, m_ref, o_ref):
    o_ref[...] = _rms_modulate(y_ref[...], g_ref[...], m_ref[0, 0:1, :], m_ref[0, 1:2, :]).astype(o_ref.dtype)


def _prenorm(y, g, mod_l, n_prompt, ts):
    n, d = y.shape
    tm = min(512, ts)
    return pl.pallas_call(
        _prenorm_body,
        grid=(n // tm,),
        in_specs=[
            pl.BlockSpec((tm, d), lambda j: (j, 0)),
            pl.BlockSpec((1, d), lambda j: (0, 0)),
            pl.BlockSpec((1, 6, d), lambda j: (_cond_of_rows(j * tm, n_prompt, ts), 0, 0)),
        ],
        out_specs=pl.BlockSpec((tm, d), lambda j: (j, 0)),
        out_shape=jax.ShapeDtypeStruct((n, d), BF16),
        compiler_params=_cparams(("parallel",)),
        name="prenorm",
    )(y, g.reshape(1, d), mod_l)


def _mm_body(x_ref, w_ref, o_ref):
    o_ref[...] = jnp.dot(x_ref[...], w_ref[...], preferred_element_type=F32).astype(o_ref.dtype)


def _matmul(x, w, out_dtype=F32):
    m, k = x.shape
    n = w.shape[1]
    tm, tn = _pick(m, 1024), _pick(n, 1024)
    return pl.pallas_call(
        _mm_body,
        grid=(m // tm, n // tn),
        in_specs=[pl.BlockSpec((tm, k), lambda i, j: (i, 0)), pl.BlockSpec((k, tn), lambda i, j: (0, j))],
        out_specs=pl.BlockSpec((tm, tn), lambda i, j: (i, j)),
        out_shape=jax.ShapeDtypeStruct((m, n), out_dtype),
        compiler_params=_cparams(("parallel", "parallel")),
        name="in_proj",
    )(x, w)


def _hgrn_gates(z, lb):
    t = jnp.exp(-jnp.abs(z))
    pos = z >= 0.0
    inv = 1.0 / (1.0 + t)
    f = jnp.where(pos, 1.0 + lb * t, lb + t) * inv
    k = (1.0 - lb) * jnp.where(pos, t, 1.0) * inv
    return jnp.log(f), k


def _split3(x):
    hi = x.astype(BF16)
    r = x - hi.astype(F32)
    mid = r.astype(BF16)
    lo = (r - mid.astype(F32)).astype(BF16)
    return hi, mid, lo


def _hgrn_body(*refs, seq, hp, dh, has_s0, has_sfin, unroll):
    zq, zff, zfb, zi, zg, lb_ref, og_ref = refs[:7]
    pos = 7
    s0_ref = None
    if has_s0:
        s0_ref = refs[pos]
        pos += 1
    ya_ref = refs[pos]
    pos += 1
    sfin_ref = None
    if has_sfin:
        sfin_ref = refs[pos]
        pos += 1
    q_s, lf_s, k_s, o_s, st_s = refs[pos:pos + 5]
    w = hp * dh
    nblk = seq // HG_BLOCK

    q_s[...] = _silu(zq[...])
    for d, zf in enumerate((zff, zfb)):
        lf, k = _hgrn_gates(zf[...], lb_ref[d:d + 1, :])
        lf_s[d] = lf
        k_s[d] = k
    for d in range(2):
        for h in range(hp):
            if has_s0:
                st_s[d * hp + h] = s0_ref[0, 0, d, h].T
            else:
                st_s[d * hp + h] = jnp.zeros((dh, dh), F32)

    row = lax.broadcasted_iota(jnp.int32, (HG_BLOCK, HG_BLOCK), 0)
    col = lax.broadcasted_iota(jnp.int32, (HG_BLOCK, HG_BLOCK), 1)
    same_half = (row // HG_HALF) == (col // HG_HALF)
    rowv = lax.broadcasted_iota(jnp.int32, (HG_BLOCK, 1), 0)
    attend = (col <= row, col >= row)
    seg = tuple(jnp.where(a & same_half, 1.0, 0.0).astype(BF16) for a in attend)
    first_half = (rowv < HG_HALF, rowv >= HG_HALF)
    first_end = (HG_HALF - 1, HG_HALF)
    second_end = (HG_BLOCK - 1, 0)

    def block(i, carry):
        for d in range(2):
            r0 = i * HG_BLOCK if d == 0 else seq - HG_BLOCK * (i + 1)
            rows = pl.ds(pl.multiple_of(r0, HG_BLOCK), HG_BLOCK)
            qb = q_s[rows, :]
            kb = k_s[d, rows, :]
            vb = zi[rows, :]
            hi, mid, lo = _split3(lf_s[d, rows, :])
            cum3 = jnp.dot(seg[d], jnp.concatenate([hi, mid, lo], axis=1), preferred_element_type=F32)
            cum = cum3[:, :w] + cum3[:, w:2 * w] + cum3[:, 2 * w:]
            l_first = cum[first_end[d]:first_end[d] + 1, :]
            l_second = cum[second_end[d]:second_end[d] + 1, :]
            l_blk = l_first + l_second
            cum_blk = cum + jnp.where(first_half[d], 0.0, l_first)
            e = cum_blk - l_first
            qe = qb * jnp.exp(e)
            ke = kb * jnp.exp(-e)
            qs = qb * jnp.exp(cum_blk)
            kend = kb * jnp.exp(l_blk - cum_blk)
            dec = jnp.exp(l_blk)
            for h in range(hp):
                sl = slice(h * dh, (h + 1) * dh)
                a = jnp.where(attend[d], _bdot_nt(qe[:, sl], ke[:, sl]), 0.0)
                st = st_s[d * hp + h]
                o_s[d, rows, sl] = _bdot(a, vb[:, sl]) + _bdot_nt(qs[:, sl], st)
                st_s[d * hp + h] = st * dec[:, sl] + _bdot_tn(vb[:, sl], kend[:, sl])
        return carry

    lax.fori_loop(0, nblk, block, 0, unroll=unroll)

    o = o_s[0] + o_s[1]
    gate = _silu(zg[...])
    for h in range(hp):
        sl = slice(h * dh, (h + 1) * dh)
        oh = o[:, sl]
        ms = jnp.mean(oh * oh, axis=-1, keepdims=True)
        ya_ref[:, sl] = (oh * lax.rsqrt(ms + NORM_EPS) * og_ref[...] * gate[:, sl]).astype(ya_ref.dtype)
    if has_sfin:
        for d in range(2):
            for h in range(hp):
                sfin_ref[0, d, h] = st_s[d * hp + h].T


def _hgrn(z, lb_l, og_l, *, row_blk0, nb, seq, heads, dh, layer, s0=None, want_state=False, hp=None, unroll=1):
    if hp is None:
        hp = next(c for c in (4, 2, 1) if heads % c == 0)
    w = hp * dh
    hw = heads * dh
    ng = heads // hp
    cb = hw // w

    def zspec(k):
        return pl.BlockSpec((seq, w), lambda b, g, k=k: (row_blk0 + b, k * cb + g))

    in_specs = [zspec(0), zspec(1), zspec(2), zspec(3), zspec(4),
                pl.BlockSpec((2, w), lambda b, g: (0, g)),
                pl.BlockSpec((1, dh), lambda b, g: (0, 0))]
    args = [z, z, z, z, z, lb_l, og_l.reshape(1, dh)]
    if s0 is not None:
        in_specs.append(pl.BlockSpec((1, 1, 2, hp, dh, dh), lambda b, g: (b, layer, 0, g, 0, 0)))
        args.append(s0)
    out_specs = [pl.BlockSpec((seq, w), lambda b, g: (b, g))]
    out_shape = [jax.ShapeDtypeStruct((nb * seq, hw), BF16)]
    if want_state:
        out_specs.append(pl.BlockSpec((1, 2, hp, dh, dh), lambda b, g: (b, 0, g, 0, 0)))
        out_shape.append(jax.ShapeDtypeStruct((nb, 2, heads, dh, dh), F32))
    res = pl.pallas_call(
        functools.partial(_hgrn_body, seq=seq, hp=hp, dh=dh, has_s0=s0 is not None, has_sfin=want_state,
                          unroll=unroll),
        grid=(nb, ng),
        in_specs=in_specs,
        out_specs=out_specs,
        out_shape=out_shape,
        scratch_shapes=[
            pltpu.VMEM((seq, w), F32),
            pltpu.VMEM((2, seq, w), F32),
            pltpu.VMEM((2, seq, w), F32),
            pltpu.VMEM((2, seq, w), F32),
            pltpu.VMEM((2 * hp, dh, dh), F32),
        ],
        compiler_params=_cparams(("parallel", "parallel")),
        name="hgrn",
    )(*args)
    return res if want_state else res[0]


def _head_rms(x, g):
    ms = jnp.mean(x * x, axis=-1, keepdims=True)
    return x * lax.rsqrt(ms + NORM_EPS) * g


def _ctx_attn_body(zq, zk, zv, qg_ref, kg_ref, yb_ref, k_ref, v_ref, *, hp, dh):
    scale = dh ** -0.5
    for h in range(hp):
        sl = slice(h * dh, (h + 1) * dh)
        q = _head_rms(zq[:, sl], qg_ref[...])
        k = _head_rms(zk[:, sl], kg_ref[...])
        v = zv[:, sl]
        k_ref[0, 0, h] = k
        v_ref[0, 0, h] = v
        s = _bdot_nt(q, k) * scale
        p = jnp.exp(s - jnp.max(s, axis=-1, keepdims=True))
        o = _bdot(p, v) / jnp.sum(p, axis=-1, keepdims=True)
        yb_ref[:, sl] = o.astype(yb_ref.dtype)


def _ctx_attention(z, qg, kg, *, nb, seq, heads, dh, col0):
    hp = 2 if heads % 2 == 0 else 1
    w = hp * dh
    nw = heads * dh
    ng = heads // hp
    c0 = col0 // w
    cb = nw // w

    def zspec(k):
        return pl.BlockSpec((seq, w), lambda b, g, k=k: (b, c0 + k * cb + g))

    kv_spec = pl.BlockSpec((1, 1, hp, seq, dh), lambda b, g: (b, 0, g, 0, 0))
    kv_shape = jax.ShapeDtypeStruct((nb, 1, heads, seq, dh), F32)
    return pl.pallas_call(
        functools.partial(_ctx_attn_body, hp=hp, dh=dh),
        grid=(nb, ng),
        in_specs=[zspec(0), zspec(1), zspec(2),
                  pl.BlockSpec((1, dh), lambda b, g: (0, 0)), pl.BlockSpec((1, dh), lambda b, g: (0, 0))],
        out_specs=[pl.BlockSpec((seq, w), lambda b, g: (b, g)), kv_spec, kv_spec],
        out_shape=[jax.ShapeDtypeStruct((nb * seq, nw), BF16), kv_shape, kv_shape],
        compiler_params=_cparams(("parallel", "parallel")),
        name="ctx_attention",
    )(z, z, z, qg.reshape(1, dh), kg.reshape(1, dh))


def _na_bias(rpb, seq):
    win_r, win_c = (rpb.shape[2] + 1) // 2, (rpb.shape[3] + 1) // 2
    rows = seq // GRID_W
    wr = min(win_r, rows)
    r, c = np.arange(rows), np.arange(GRID_W)
    r0 = np.clip(r - wr // 2, 0, rows - wr)
    c0 = np.clip(c - win_c // 2, 0, GRID_W - win_c)
    row_ok = (r[None, :] >= r0[:, None]) & (r[None, :] < r0[:, None] + wr)
    col_ok = (c[None, :] >= c0[:, None]) & (c[None, :] < c0[:, None] + win_c)
    valid = (row_ok[:, None, :, None] & col_ok[None, :, None, :]).reshape(seq, seq)
    sel_r = (r[None, :, None] - r[:, None, None] + win_r - 1 == np.arange(2 * win_r - 1)).astype(np.float32)
    sel_c = (c[None, :, None] - c[:, None, None] + win_c - 1 == np.arange(2 * win_c - 1)).astype(np.float32)
    table = jnp.einsum("xka,lhab,cqb->lhxckq", sel_r, rpb.astype(F32), sel_c, precision=lax.Precision.HIGHEST)
    table = table.reshape(rpb.shape[0], rpb.shape[1], seq, seq)
    return jnp.where(valid[None, None], table, NEG_INF)


def _na_attn_body(zq, zk, zv, kc_ref, vc_ref, bias_ref, qg_ref, kg_ref, yb_ref, *, seq, dh, tq):
    scale = dh ** -0.5
    q = _head_rms(zq[...], qg_ref[...]).astype(BF16)
    k = _head_rms(zk[...], kg_ref[...]).astype(BF16)
    v = zv[...].astype(BF16)
    kc = kc_ref[0, 0, 0].astype(BF16)
    vc = vc_ref[0, 0, 0].astype(BF16)
    for i in range(seq // tq):
        rows = slice(i * tq, (i + 1) * tq)
        qt = q[rows]
        s_win = _bdot_nt(qt, k) * scale + bias_ref[0, rows, :]
        s_ctx = _bdot_nt(qt, kc) * scale
        m = jnp.maximum(jnp.max(s_win, axis=-1, keepdims=True), jnp.max(s_ctx, axis=-1, keepdims=True))
        p_win = jnp.exp(s_win - m)
        p_ctx = jnp.exp(s_ctx - m)
        den = jnp.sum(p_win, axis=-1, keepdims=True) + jnp.sum(p_ctx, axis=-1, keepdims=True)
        o = (_bdot(p_win, v) + _bdot(p_ctx, vc)) / den
        yb_ref[rows, :] = o.astype(yb_ref.dtype)


def _na_attention(z, cache_k, cache_v, bias, qg, kg, *, row_blk0, nb, seq, heads, dh, col0, layer):
    nw = heads * dh
    c0 = col0 // dh
    past = cache_k.shape[3]

    def zspec(k):
        return pl.BlockSpec((seq, dh), lambda h, b, k=k: (row_blk0 + b, c0 + k * heads + h))

    cache_spec = pl.BlockSpec((1, 1, 1, past, dh), lambda h, b: (b, layer, h, 0, 0))
    return pl.pallas_call(
        functools.partial(_na_attn_body, seq=seq, dh=dh, tq=min(256, seq)),
        grid=(heads, nb),
        in_specs=[zspec(0), zspec(1), zspec(2), cache_spec, cache_spec,
                  pl.BlockSpec((1, seq, seq), lambda h, b: (h, 0, 0)),
                  pl.BlockSpec((1, dh), lambda h, b: (0, 0)), pl.BlockSpec((1, dh), lambda h, b: (0, 0))],
        out_specs=pl.BlockSpec((seq, dh), lambda h, b: (b, h)),
        out_shape=jax.ShapeDtypeStruct((nb * seq, nw), BF16),
        compiler_params=_cparams(("parallel", "parallel")),
        name="na_attention",
    )(z, z, z, cache_k, cache_v, bias, qg.reshape(1, dh), kg.reshape(1, dh))


def _merge_body(ya_ref, yb_ref, wh_ref, wn_ref, ga_ref, gb_ref, o_ref):
    a = jnp.dot(ya_ref[...], wh_ref[...], preferred_element_type=F32)
    b = jnp.dot(yb_ref[...], wn_ref[...], preferred_element_type=F32)
    o_ref[...] = (jax.nn.sigmoid(ga_ref[...]) * a + jax.nn.sigmoid(gb_ref[...]) * b).astype(o_ref.dtype)


def _merge(ya, yb, w_hb, w_nb, z, col_ga):
    n, hw = ya.shape
    nw = yb.shape[1]
    d = w_hb.shape[1]
    tm, tn = _pick(n, 512), _pick(d, 1024)
    ca, cbk = col_ga // tn, (col_ga + d) // tn
    return pl.pallas_call(
        _merge_body,
        grid=(n // tm, d // tn),
        in_specs=[
            pl.BlockSpec((tm, hw), lambda i, j: (i, 0)),
            pl.BlockSpec((tm, nw), lambda i, j: (i, 0)),
            pl.BlockSpec((hw, tn), lambda i, j: (0, j)),
            pl.BlockSpec((nw, tn), lambda i, j: (0, j)),
            pl.BlockSpec((tm, tn), lambda i, j: (i, ca + j)),
            pl.BlockSpec((tm, tn), lambda i, j: (i, cbk + j)),
        ],
        out_specs=pl.BlockSpec((tm, tn), lambda i, j: (i, j)),
        out_shape=jax.ShapeDtypeStruct((n, d), BF16),
        compiler_params=_cparams(("parallel", "parallel")),
        name="merge",
    )(ya, yb, w_hb, w_nb, z, z)


def _outproj_body(*refs, n_experts):
    m_ref, w_ref, y_ref, mod_ref, g_ref = refs[:5]
    if n_experts:
        r_ref, y1_ref, h2_ref, gw_ref, gi_ref = refs[5:]
    else:
        y1_ref, h2_ref = refs[5:]
    y1 = y_ref[...] + mod_ref[0, 2:3, :] * jnp.dot(m_ref[...], w_ref[...], preferred_element_type=F32)
    y1_ref[...] = y1
    h2 = _rms_modulate(y1, g_ref[...], mod_ref[0, 3:4, :], mod_ref[0, 4:5, :])
    h2_ref[...] = h2.astype(h2_ref.dtype)
    if n_experts:
        r = r_ref[...]
        h_hi = h2.astype(BF16)
        h_lo = (h2 - h_hi.astype(F32)).astype(BF16)
        r_hi = r.astype(BF16)
        r_lo = (r - r_hi.astype(F32)).astype(BF16)
        logits = (jnp.dot(h_hi, r_hi, preferred_element_type=F32)
                  + jnp.dot(h_hi, r_lo, preferred_element_type=F32)
                  + jnp.dot(h_lo, r_hi, preferred_element_type=F32))
        lane = lax.broadcasted_iota(jnp.int32, logits.shape, 1).astype(F32)
        big = float(LANES)
        lg = jnp.where(lane < n_experts, logits, -jnp.inf)
        m1 = jnp.max(lg, axis=-1, keepdims=True)
        i1 = jnp.min(jnp.where(lg == m1, lane, big), axis=-1, keepdims=True)
        lg2 = jnp.where(lane == i1, -jnp.inf, lg)
        m2 = jnp.max(lg2, axis=-1, keepdims=True)
        i2 = jnp.min(jnp.where(lg2 == m2, lane, big), axis=-1, keepdims=True)
        t = jnp.exp(m2 - m1)
        w1 = 1.0 / (1.0 + t)
        w2 = t / (1.0 + t)
        gw_ref[...] = jnp.where(lane == 0.0, w1, jnp.where(lane == 1.0, w2, 0.0))
        gi_ref[...] = jnp.where(lane == 0.0, i1, jnp.where(lane == 1.0, i2, 0.0)).astype(jnp.int32)


def _outproj(m, w_out, y, mod_l, g2, n_prompt, ts, router=None):
    n, d = y.shape
    tm = min(256, ts)
    n_experts = 0 if router is None else router.shape[1]
    in_specs = [
        pl.BlockSpec((tm, d), lambda j: (j, 0)),
        pl.BlockSpec((d, d), lambda j: (0, 0)),
        pl.BlockSpec((tm, d), lambda j: (j, 0)),
        pl.BlockSpec((1, 6, d), lambda j: (_cond_of_rows(j * tm, n_prompt, ts), 0, 0)),
        pl.BlockSpec((1, d), lambda j: (0, 0)),
    ]
    args = [m, w_out, y, mod_l, g2.reshape(1, d)]
    row_spec = pl.BlockSpec((tm, d), lambda j: (j, 0))
    out_specs = [row_spec, row_spec]
    out_shape = [jax.ShapeDtypeStruct((n, d), F32), jax.ShapeDtypeStruct((n, d), F32 if n_experts else BF16)]
    if n_experts:
        in_specs.append(pl.BlockSpec((d, LANES), lambda j: (0, 0)))
        args.append(jnp.pad(router, ((0, 0), (0, LANES - n_experts))))
        lane_spec = pl.BlockSpec((tm, LANES), lambda j: (j, 0))
        out_specs += [lane_spec, lane_spec]
        out_shape += [jax.ShapeDtypeStruct((n, LANES), F32), jax.ShapeDtypeStruct((n, LANES), jnp.int32)]
    return pl.pallas_call(
        functools.partial(_outproj_body, n_experts=n_experts),
        grid=(n // tm,),
        in_specs=in_specs,
        out_specs=out_specs,
        out_shape=out_shape,
        compiler_params=_cparams(("parallel",)),
        name="outproj",
    )(*args)


def _ffn_body(te_ref, nt_ref, *refs, residual):
    del te_ref
    if residual:
        x_ref, wg_ref, wu_ref, wd_ref, y_ref, mod_ref, o_ref = refs
    else:
        x_ref, wg_ref, wu_ref, wd_ref, o_ref = refs
    j, f = pl.program_id(0), pl.program_id(1)
    nf = pl.num_programs(1)

    @pl.when(j < nt_ref[0])
    def _():
        x = x_ref[...].astype(BF16)
        g = jnp.dot(x, wg_ref[0], preferred_element_type=F32)
        u = jnp.dot(x, wu_ref[0], preferred_element_type=F32)
        part = jnp.dot((_silu(g) * u).astype(BF16), wd_ref[0], preferred_element_type=F32)

        @pl.when(f == 0)
        def _():
            o_ref[...] = part

        @pl.when(f > 0)
        def _():
            o_ref[...] += part

        if residual:
            @pl.when(f == nf - 1)
            def _():
                o_ref[...] = y_ref[...] + mod_ref[0, 5:6, :] * o_ref[...]

    @pl.when((j >= nt_ref[0]) & (f == 0))
    def _():
        o_ref[...] = jnp.zeros_like(o_ref)


def _ffn(x, wg, wu, wd, tile_expert, n_tiles, *, tm, residual=None):
    n, d = x.shape
    ff = wg.shape[2]
    tf = _pick(ff, 512)
    nf = ff // tf

    def row_map(j, f, te, nt):
        return (jnp.minimum(j, nt[0] - 1), 0)

    def ff_idx(j, f, nt):
        return jnp.where(j < nt[0], f, nf - 1)

    def e_idx(j, te, nt):
        return te[jnp.minimum(j, nt[0] - 1)]

    in_specs = [
        pl.BlockSpec((tm, d), row_map),
        pl.BlockSpec((1, d, tf), lambda j, f, te, nt: (e_idx(j, te, nt), 0, ff_idx(j, f, nt))),
        pl.BlockSpec((1, d, tf), lambda j, f, te, nt: (e_idx(j, te, nt), 0, ff_idx(j, f, nt))),
        pl.BlockSpec((1, tf, d), lambda j, f, te, nt: (e_idx(j, te, nt), ff_idx(j, f, nt), 0)),
    ]
    args = [x, wg, wu, wd]
    if residual is not None:
        y, mod_l, n_prompt, ts = residual
        in_specs += [
            pl.BlockSpec((tm, d), row_map),
            pl.BlockSpec((1, 6, d), lambda j, f, te, nt: (_cond_of_rows(j * tm, n_prompt, ts), 0, 0)),
        ]
        args += [y, mod_l]
    return pl.pallas_call(
        functools.partial(_ffn_body, residual=residual is not None),
        grid_spec=pltpu.PrefetchScalarGridSpec(
            num_scalar_prefetch=2,
            grid=(n // tm, nf),
            in_specs=in_specs,
            out_specs=pl.BlockSpec((tm, d), lambda j, f, te, nt: (j, 0)),
        ),
        out_shape=jax.ShapeDtypeStruct((n, d), F32),
        compiler_params=_cparams(("arbitrary", "arbitrary")),
        name="swiglu",
    )(tile_expert, n_tiles, *args)


def _dispatch_body(p0_ref, p1_ref, src, dst_in, dst, sem, *, tb):
    del dst_in

    def issue(t, c):
        row = src.at[pl.ds(t, 1)]
        pltpu.make_async_copy(row, dst.at[pl.ds(p0_ref[0, 0, t], 1)], sem).start()
        pltpu.make_async_copy(row, dst.at[pl.ds(p1_ref[0, 0, t], 1)], sem).start()
        return c

    lax.fori_loop(0, tb, issue, 0)
    for _ in range(2):
        pltpu.make_async_copy(src, dst.at[pl.ds(0, tb)], sem).wait()


def _dispatch(h, pos0, pos1, n_sorted, tb):
    n, d = h.shape
    return pl.pallas_call(
        functools.partial(_dispatch_body, tb=tb),
        grid=(n // tb,),
        in_specs=[
            pl.BlockSpec((1, 1, tb), lambda i: (i, 0, 0), memory_space=pltpu.SMEM),
            pl.BlockSpec((1, 1, tb), lambda i: (i, 0, 0), memory_space=pltpu.SMEM),
            pl.BlockSpec((tb, d), lambda i: (i, 0)),
            pl.BlockSpec(memory_space=pl.ANY),
        ],
        out_specs=pl.BlockSpec(memory_space=pl.ANY),
        out_shape=jax.ShapeDtypeStruct((n_sorted, d), h.dtype),
        scratch_shapes=[pltpu.SemaphoreType.DMA(())],
        input_output_aliases={3: 0},
        compiler_params=_cparams(("arbitrary",)),
        name="dispatch",
    )(pos0.reshape(n // tb, 1, tb), pos1.reshape(n // tb, 1, tb), h, jnp.zeros((n_sorted, d), h.dtype))


def _combine_body(p0_ref, p1_ref, o_hbm, y_ref, gw_ref, mod_ref, out_ref, buf0, buf1, sem, *, tc):
    def issue(t, c):
        pltpu.make_async_copy(o_hbm.at[pl.ds(p0_ref[0, 0, t], 1)], buf0.at[pl.ds(t, 1)], sem).start()
        pltpu.make_async_copy(o_hbm.at[pl.ds(p1_ref[0, 0, t], 1)], buf1.at[pl.ds(t, 1)], sem).start()
        return c

    lax.fori_loop(0, tc, issue, 0)
    pltpu.make_async_copy(o_hbm.at[pl.ds(0, tc)], buf0, sem).wait()
    pltpu.make_async_copy(o_hbm.at[pl.ds(0, tc)], buf1, sem).wait()
    gw = gw_ref[...]
    mix = gw[:, 0:1] * buf0[...] + gw[:, 1:2] * buf1[...]
    out_ref[...] = y_ref[...] + mod_ref[0, 5:6, :] * mix


def _combine(o_sorted, pos0, pos1, y, gw, mod_l, n_prompt, ts):
    n, d = y.shape
    tc = min(256, ts)
    pos0, pos1 = pos0.reshape(n // tc, 1, tc), pos1.reshape(n // tc, 1, tc)
    return pl.pallas_call(
        functools.partial(_combine_body, tc=tc),
        grid=(n // tc,),
        in_specs=[
            pl.BlockSpec((1, 1, tc), lambda i: (i, 0, 0), memory_space=pltpu.SMEM),
            pl.BlockSpec((1, 1, tc), lambda i: (i, 0, 0), memory_space=pltpu.SMEM),
            pl.BlockSpec(memory_space=pl.ANY),
            pl.BlockSpec((tc, d), lambda i: (i, 0)),
            pl.BlockSpec((tc, LANES), lambda i: (i, 0)),
            pl.BlockSpec((1, 6, d), lambda i: (_cond_of_rows(i * tc, n_prompt, ts), 0, 0)),
        ],
        out_specs=pl.BlockSpec((tc, d), lambda i: (i, 0)),
        out_shape=jax.ShapeDtypeStruct((n, d), F32),
        scratch_shapes=[pltpu.VMEM((tc, d), F32), pltpu.VMEM((tc, d), F32), pltpu.SemaphoreType.DMA(())],
        compiler_params=_cparams(("arbitrary",)),
        name="combine",
    )(pos0, pos1, o_sorted, y, gw, mod_l)


def _routing_tables(top_i, n_experts, tm, n_tiles_max):
    n = top_i.shape[0]
    flat = jnp.concatenate([top_i[:, 0], top_i[:, 1]])
    onehot = (flat[:, None] == jnp.arange(n_experts, dtype=jnp.int32)[None, :]).astype(jnp.int32)
    csum = jnp.cumsum(onehot, axis=0)
    rank = jnp.sum((csum - onehot) * onehot, axis=1)
    counts = csum[-1]
    tiles = (counts + tm - 1) // tm
    tile_end = jnp.cumsum(tiles)
    start = (tile_end - tiles) * tm
    pos = jnp.sum(onehot * start[None, :], axis=1) + rank
    n_tiles = tile_end[-1]
    j = jnp.arange(n_tiles_max, dtype=jnp.int32)
    tile_expert = jnp.sum((j[:, None] >= tile_end[None, :]).astype(jnp.int32), axis=1)
    tile_expert = jnp.minimum(tile_expert, n_experts - 1).astype(jnp.int32)
    return pos[:n].astype(jnp.int32), pos[n:].astype(jnp.int32), tile_expert, n_tiles.reshape(1).astype(jnp.int32)


def kernel(x_prompt, x_sample, cache_k, cache_v, state_hgrn, c, c_ctx, norm1_g, norm2_g, w_ada, b_ada, w_in,
           hg_lb, hg_onorm_g, na_qn_g, na_kn_g, na_rpb, w_hb, w_nb, w_out, ffn_wg, ffn_wu, ffn_wd,
           moe_router, moe_wg, moe_wu, moe_wd):
    bp, tp, d = x_prompt.shape
    bs, ts, _ = x_sample.shape
    depth = w_in.shape[0]
    hg_heads, dh = state_hgrn.shape[3], state_hgrn.shape[4]
    na_heads = cache_k.shape[2]
    hw, nw = hg_heads * dh, na_heads * dh
    n_experts = moe_wg.shape[1]
    n_prompt, n_sample = bp * tp, bs * ts
    n = n_prompt + n_sample
    assert n_prompt % ts == 0 and ts % tp == 0 and bs + 1 <= N_COND
    col_nq = 5 * hw
    col_ga = 5 * hw + 3 * nw

    p = jax.nn.softmax(hg_lb.astype(F32), axis=1)
    cs = jnp.cumsum(p, axis=1)
    lbs = cs - cs[:, :1]
    bias = _na_bias(na_rpb, ts)
    w_in_b, w_hb_b, w_nb_b, w_out_b = (a.astype(BF16) for a in (w_in, w_hb, w_nb, w_out))
    ffn_b = tuple(a.astype(BF16) for a in (ffn_wg, ffn_wu, ffn_wd))
    moe_b = tuple(a.astype(BF16) for a in (moe_wg, moe_wu, moe_wd))

    cond = jnp.concatenate([c_ctx[None, :], c, jnp.zeros((N_COND - 1 - bs, d), F32)], axis=0)
    mod = _modulation(cond, w_ada, b_ada).reshape(depth, N_COND, 6, d)

    y = jnp.concatenate([x_prompt.reshape(n_prompt, d), x_sample.reshape(n_sample, d)], axis=0)
    tm_moe = min(512, n)
    n_tiles_max = (2 * n) // tm_moe + n_experts
    ks_out, vs_out, ss_out = [], [], []
    for l in range(depth):
        h = _prenorm(y, norm1_g[l], mod[l], n_prompt, ts)
        z = _matmul(h, w_in_b[l])
        ya_p, s_ctx = _hgrn(z, lbs[:, l], hg_onorm_g[l], row_blk0=0, nb=bp, seq=tp, heads=hg_heads, dh=dh,
                            layer=l, want_state=True)
        ya_s = _hgrn(z, lbs[:, l], hg_onorm_g[l], row_blk0=n_prompt // ts, nb=bs, seq=ts, heads=hg_heads,
                     dh=dh, layer=l, s0=state_hgrn)
        yb_p, k_new, v_new = _ctx_attention(z, na_qn_g[l], na_kn_g[l], nb=bp, seq=tp, heads=na_heads, dh=dh,
                                            col0=col_nq)
        yb_s = _na_attention(z, cache_k, cache_v, bias[l], na_qn_g[l], na_kn_g[l], row_blk0=n_prompt // ts,
                             nb=bs, seq=ts, heads=na_heads, dh=dh, col0=col_nq, layer=l)
        ya = jnp.concatenate([ya_p, ya_s], axis=0)
        yb = jnp.concatenate([yb_p, yb_s], axis=0)
        m = _merge(ya, yb, w_hb_b[l], w_nb_b[l], z, col_ga)
        i = l // 2
        if l % 2 == 0:
            y1, h2 = _outproj(m, w_out_b[l], y, mod[l], norm2_g[l], n_prompt, ts)
            tm = min(512, ts)
            y = _ffn(h2, ffn_b[0][i][None], ffn_b[1][i][None], ffn_b[2][i][None],
                     jnp.zeros((n // tm,), jnp.int32), jnp.full((1,), n // tm, jnp.int32), tm=tm,
                     residual=(y1, mod[l], n_prompt, ts))
        else:
            y1, h2, gw, gi = _outproj(m, w_out_b[l], y, mod[l], norm2_g[l], n_prompt, ts, router=moe_router[i])
            pos0, pos1, tile_expert, n_tiles = _routing_tables(gi[:, :2], n_experts, tm_moe, n_tiles_max)
            xs = _dispatch(h2, pos0, pos1, n_tiles_max * tm_moe, min(1024, ts))
            o_sorted = _ffn(xs, moe_b[0][i], moe_b[1][i], moe_b[2][i], tile_expert, n_tiles, tm=tm_moe)
            y = _combine(o_sorted, pos0, pos1, y1, gw, mod[l], n_prompt, ts)
        ks_out.append(k_new)
        vs_out.append(v_new)
        ss_out.append(s_ctx)

    y_prompt = y[:n_prompt].reshape(bp, tp, d)
    y_sample = y[n_prompt:].reshape(bs, ts, d)
    new_cache_k = jnp.concatenate(ks_out, axis=1)
    new_cache_v = jnp.concatenate(vs_out, axis=1)
    new_state_hgrn = jnp.stack(ss_out, axis=1)
    return (y_prompt, y_sample, new_cache_k, new_cache_v, new_state_hgrn)
```

```python
import functools

import numpy as np
import jax
import jax.numpy as jnp
from jax import lax
from jax.experimental import pallas as pl
from jax.experimental.pallas import tpu as pltpu

F32 = jnp.float32
BF16 = jnp.bfloat16

NORM_EPS = 1e-6
NEG_INF = -1e30
GRID_W = 64
NA_QB = 16
LANES = 128
N_COND = 16
HG_BLOCK = 32
HG_HALF = HG_BLOCK // 2
VMEM_LIMIT = 56 * 1024 * 1024


def _cparams(semantics, vmem=VMEM_LIMIT):
    return pltpu.CompilerParams(dimension_semantics=semantics, vmem_limit_bytes=vmem)


def _silu(x):
    return x * jax.nn.sigmoid(x)


def _bdot(a, b):
    return jnp.dot(a.astype(BF16), b.astype(BF16), preferred_element_type=F32)


def _bdot_nt(a, b):
    return lax.dot_general(a.astype(BF16), b.astype(BF16), (((1,), (1,)), ((), ())),
                           preferred_element_type=F32)


def _bdot_tn(a, b):
    return lax.dot_general(a.astype(BF16), b.astype(BF16), (((0,), (0,)), ((), ())),
                           preferred_element_type=F32)


def _pick(n, pref):
    if n <= pref:
        return n
    t = pref - pref % LANES
    while n % t:
        t -= LANES
    return t


def _cond_of_rows(row0, n_prompt, ts):
    return jnp.where(row0 < n_prompt, 0, 1 + (row0 - n_prompt) // ts)


def _mod_body(c_ref, w_ref, b_ref, o_ref):
    o_ref[0] = _bdot(_silu(c_ref[...]), w_ref[0]) + b_ref[0]


def _modulation(cond, w_ada, b_ada):
    depth, d, n6 = w_ada.shape
    tn = _pick(n6, 1024)
    return pl.pallas_call(
        _mod_body,
        grid=(depth, n6 // tn),
        in_specs=[
            pl.BlockSpec((N_COND, d), lambda l, j: (0, 0)),
            pl.BlockSpec((1, d, tn), lambda l, j: (l, 0, j)),
            pl.BlockSpec((1, 1, tn), lambda l, j: (l, 0, j)),
        ],
        out_specs=pl.BlockSpec((1, N_COND, tn), lambda l, j: (l, 0, j)),
        out_shape=jax.ShapeDtypeStruct((depth, N_COND, n6), F32),
        compiler_params=_cparams(("parallel", "parallel")),
        name="modulation",
    )(cond, w_ada, b_ada.reshape(depth, 1, n6))


def _rms_modulate(x, g, shift, scale):
    ms = jnp.mean(x * x, axis=-1, keepdims=True)
    return (x * lax.rsqrt(ms + NORM_EPS) * g) * (1.0 + scale) + shift


def _stream_specs(y, tm, n_prompt):
    if not isinstance(y, tuple):
        return [pl.BlockSpec((tm, y.shape[1]), lambda j: (j, 0))], [y]
    d = y[0].shape[1]
    npt = n_prompt // tm
    return ([pl.BlockSpec((tm, d), lambda j: (jnp.minimum(j, npt - 1), 0)),
             pl.BlockSpec((tm, d), lambda j: (jnp.maximum(j - npt, 0), 0))], list(y))


def _stream_tile(refs, npt):
    if len(refs) == 1:
        return refs[0][...]
    return jnp.where(pl.program_id(0) < npt, refs[0][...], refs[1][...])


def _prenorm_body(*refs, npt):
    g_ref, m_ref, o_ref = refs[-3:]
    y = _stream_tile(refs[:-3], npt)
    o_ref[...] = _rms_modulate(y, g_ref[...], m_ref[0, 0:1, :], m_ref[0, 1:2, :]).astype(o_ref.dtype)


def _prenorm(y, g, mod_l, n, n_prompt, ts):
    d = g.shape[0]
    tm = min(512, ts)
    y_specs, y_args = _stream_specs(y, tm, n_prompt)
    return pl.pallas_call(
        functools.partial(_prenorm_body, npt=n_prompt // tm),
        grid=(n // tm,),
        in_specs=y_specs + [
            pl.BlockSpec((1, d), lambda j: (0, 0)),
            pl.BlockSpec((1, 6, d), lambda j: (_cond_of_rows(j * tm, n_prompt, ts), 0, 0)),
        ],
        out_specs=pl.BlockSpec((tm, d), lambda j: (j, 0)),
        out_shape=jax.ShapeDtypeStruct((n, d), BF16),
        compiler_params=_cparams(("parallel",)),
        name="prenorm",
    )(*y_args, g.reshape(1, d), mod_l)


def _mm_body(x_ref, w_ref, o_ref):
    o_ref[...] = jnp.dot(x_ref[...], w_ref[...], preferred_element_type=F32).astype(o_ref.dtype)


def _matmul(x, w, out_dtype=F32):
    m, k = x.shape
    n = w.shape[1]
    tm, tn = _pick(m, 1024), _pick(n, 1024)
    return pl.pallas_call(
        _mm_body,
        grid=(m // tm, n // tn),
        in_specs=[pl.BlockSpec((tm, k), lambda i, j: (i, 0)), pl.BlockSpec((k, tn), lambda i, j: (0, j))],
        out_specs=pl.BlockSpec((tm, tn), lambda i, j: (i, j)),
        out_shape=jax.ShapeDtypeStruct((m, n), out_dtype),
        compiler_params=_cparams(("parallel", "parallel")),
        name="in_proj",
    )(x, w)


def _hgrn_gates(z, lb):
    t = jnp.exp(-jnp.abs(z))
    pos = z >= 0.0
    inv = 1.0 / (1.0 + t)
    f = jnp.where(pos, 1.0 + lb * t, lb + t) * inv
    k = (1.0 - lb) * jnp.where(pos, t, 1.0) * inv
    return jnp.log(f), k


def _split2(x):
    hi = x.astype(BF16)
    lo = (x - hi.astype(F32)).astype(BF16)
    return hi, lo


def _hgrn_body(*refs, seq, hp, dh, has_s0, has_sfin, unroll):
    zq, zff, zfb, zi, zg, lb_ref, og_ref = refs[:7]
    pos = 7
    s0_ref = None
    if has_s0:
        s0_ref = refs[pos]
        pos += 1
    ya_ref = refs[pos]
    pos += 1
    sfin_ref = None
    if has_sfin:
        sfin_ref = refs[pos]
        pos += 1
    o_s, st_s = refs[pos:pos + 2]
    w = hp * dh
    nblk = seq // HG_BLOCK
    zf = (zff, zfb)

    for d in range(2):
        for h in range(hp):
            if has_s0:
                st_s[d * hp + h] = s0_ref[0, 0, d, h].T
            else:
                st_s[d * hp + h] = jnp.zeros((dh, dh), F32)

    row = lax.broadcasted_iota(jnp.int32, (HG_BLOCK, HG_BLOCK), 0)
    col = lax.broadcasted_iota(jnp.int32, (HG_BLOCK, HG_BLOCK), 1)
    same_half = (row // HG_HALF) == (col // HG_HALF)
    rowv = lax.broadcasted_iota(jnp.int32, (HG_BLOCK, 1), 0)
    attend = (col <= row, col >= row)
    seg = tuple(jnp.where(a & same_half, 1.0, 0.0).astype(BF16) for a in attend)
    first_half = (rowv < HG_HALF, rowv >= HG_HALF)
    first_end = (HG_HALF - 1, HG_HALF)
    second_end = (HG_BLOCK - 1, 0)

    units = [(d, h) for d in range(2) for h in range(hp)]

    def block(i, carry):
        rows = (pl.ds(pl.multiple_of(i * HG_BLOCK, HG_BLOCK), HG_BLOCK),
                pl.ds(pl.multiple_of(seq - HG_BLOCK * (i + 1), HG_BLOCK), HG_BLOCK))
        cum2, kbs = [], []
        for d in range(2):
            lf, kb = _hgrn_gates(zf[d][rows[d], :], lb_ref[d:d + 1, :])
            kbs.append(kb)
            hi, lo = _split2(lf)
            cum2.append(jnp.dot(seg[d], jnp.concatenate([hi, lo], axis=1), preferred_element_type=F32))
        vb = [zi[rows[d], :].astype(BF16) for d in range(2)]
        qe, ke, qs, kend, dec = [], [], [], [], []
        for d in range(2):
            qb = _silu(zq[rows[d], :])
            kb = kbs[d]
            cum = cum2[d][:, :w] + cum2[d][:, w:]
            l_first = cum[first_end[d]:first_end[d] + 1, :]
            l_second = cum[second_end[d]:second_end[d] + 1, :]
            l_blk = l_first + l_second
            cum_blk = cum + jnp.where(first_half[d], 0.0, l_first)
            e = cum_blk - l_first
            qe.append((qb * jnp.exp(e)).astype(BF16))
            ke.append((kb * jnp.exp(-e)).astype(BF16))
            qs.append((qb * jnp.exp(cum_blk)).astype(BF16))
            kend.append((kb * jnp.exp(l_blk - cum_blk)).astype(BF16))
            dec.append(jnp.exp(l_blk))
        sls = [slice(h * dh, (h + 1) * dh) for h in range(hp)]
        a = [_bdot_nt(qe[d][:, sls[h]], ke[d][:, sls[h]]) for d, h in units]
        st = [st_s[d * hp + h] for d, h in units]
        inter = [_bdot_nt(qs[d][:, sls[h]], st[u]) for u, (d, h) in enumerate(units)]
        ds = [_bdot_tn(vb[d][:, sls[h]], kend[d][:, sls[h]]) for d, h in units]
        intra = [_bdot(jnp.where(attend[d], a[u], 0.0), vb[d][:, sls[h]]) for u, (d, h) in enumerate(units)]
        for u, (d, h) in enumerate(units):
            st_s[d * hp + h] = st[u] * dec[d][:, sls[h]] + ds[u]
            o_s[d, rows[d], sls[h]] = intra[u] + inter[u]
        return carry

    lax.fori_loop(0, nblk, block, 0, unroll=unroll)

    def finish(c, carry):
        rows = pl.ds(pl.multiple_of(c * HG_BLOCK, HG_BLOCK), HG_BLOCK)
        o = o_s[0, rows, :] + o_s[1, rows, :]
        gate = _silu(zg[rows, :])
        for h in range(hp):
            sl = slice(h * dh, (h + 1) * dh)
            oh = o[:, sl]
            ms = jnp.mean(oh * oh, axis=-1, keepdims=True)
            ya_ref[rows, sl] = (oh * lax.rsqrt(ms + NORM_EPS) * og_ref[...] * gate[:, sl]).astype(ya_ref.dtype)
        return carry

    lax.fori_loop(0, nblk, finish, 0)
    if has_sfin:
        for d in range(2):
            for h in range(hp):
                sfin_ref[0, 0, d, h] = st_s[d * hp + h].T


def _hgrn(z, lb_l, og_l, *, row_blk0, nb, seq, heads, dh, layer, s0=None, want_state=False, hp=None, unroll=2):
    if hp is None:
        hp = next(c for c in (4, 2, 1) if heads % c == 0)
    w = hp * dh
    hw = heads * dh
    ng = heads // hp
    cb = hw // w

    def zspec(k):
        return pl.BlockSpec((seq, w), lambda b, g, k=k: (row_blk0 + b, k * cb + g))

    in_specs = [zspec(0), zspec(1), zspec(2), zspec(3), zspec(4),
                pl.BlockSpec((2, w), lambda b, g: (0, g)),
                pl.BlockSpec((1, dh), lambda b, g: (0, 0))]
    args = [z, z, z, z, z, lb_l, og_l.reshape(1, dh)]
    if s0 is not None:
        in_specs.append(pl.BlockSpec((1, 1, 2, hp, dh, dh), lambda b, g: (b, layer, 0, g, 0, 0)))
        args.append(s0)
    out_specs = [pl.BlockSpec((seq, w), lambda b, g: (b, g))]
    out_shape = [jax.ShapeDtypeStruct((nb * seq, hw), BF16)]
    if want_state:
        out_specs.append(pl.BlockSpec((1, 1, 2, hp, dh, dh), lambda b, g: (b, 0, 0, g, 0, 0)))
        out_shape.append(jax.ShapeDtypeStruct((nb, 1, 2, heads, dh, dh), F32))
    res = pl.pallas_call(
        functools.partial(_hgrn_body, seq=seq, hp=hp, dh=dh, has_s0=s0 is not None, has_sfin=want_state,
                          unroll=unroll),
        grid=(nb, ng),
        in_specs=in_specs,
        out_specs=out_specs,
        out_shape=out_shape,
        scratch_shapes=[
            pltpu.VMEM((2, seq, w), F32),
            pltpu.VMEM((2 * hp, dh, dh), F32),
        ],
        compiler_params=_cparams(("parallel", "parallel")),
        name="hgrn",
    )(*args)
    return res if want_state else res[0]


def _head_rms(x, g):
    ms = jnp.mean(x * x, axis=-1, keepdims=True)
    return x * lax.rsqrt(ms + NORM_EPS) * g


def _ctx_attn_body(zq, zk, zv, qg_ref, kg_ref, yb_ref, k_ref, v_ref, *, hp, dh):
    scale = dh ** -0.5
    for h in range(hp):
        sl = slice(h * dh, (h + 1) * dh)
        q = _head_rms(zq[:, sl], qg_ref[...])
        k = _head_rms(zk[:, sl], kg_ref[...])
        v = zv[:, sl]
        k_ref[0, 0, h] = k
        v_ref[0, 0, h] = v
        s = _bdot_nt(q, k) * scale
        p = jnp.exp(s - jnp.max(s, axis=-1, keepdims=True))
        o = _bdot(p, v) / jnp.sum(p, axis=-1, keepdims=True)
        yb_ref[:, sl] = o.astype(yb_ref.dtype)


def _ctx_attention(z, qg, kg, *, nb, seq, heads, dh, col0):
    hp = 2 if heads % 2 == 0 else 1
    w = hp * dh
    nw = heads * dh
    ng = heads // hp
    c0 = col0 // w
    cb = nw // w

    def zspec(k):
        return pl.BlockSpec((seq, w), lambda b, g, k=k: (b, c0 + k * cb + g))

    kv_spec = pl.BlockSpec((1, 1, hp, seq, dh), lambda b, g: (b, 0, g, 0, 0))
    kv_shape = jax.ShapeDtypeStruct((nb, 1, heads, seq, dh), F32)
    return pl.pallas_call(
        functools.partial(_ctx_attn_body, hp=hp, dh=dh),
        grid=(nb, ng),
        in_specs=[zspec(0), zspec(1), zspec(2),
                  pl.BlockSpec((1, dh), lambda b, g: (0, 0)), pl.BlockSpec((1, dh), lambda b, g: (0, 0))],
        out_specs=[pl.BlockSpec((seq, w), lambda b, g: (b, g)), kv_spec, kv_spec],
        out_shape=[jax.ShapeDtypeStruct((nb * seq, nw), BF16), kv_shape, kv_shape],
        compiler_params=_cparams(("parallel", "parallel")),
        name="ctx_attention",
    )(z, z, z, qg.reshape(1, dh), kg.reshape(1, dh))


def _na_bias(rpb, seq):
    win_r, win_c = (rpb.shape[2] + 1) // 2, (rpb.shape[3] + 1) // 2
    rows = seq // GRID_W
    wr = min(win_r, rows)
    r, c = np.arange(rows), np.arange(GRID_W)
    r0 = np.clip(r - wr // 2, 0, rows - wr)
    c0 = np.clip(c - win_c // 2, 0, GRID_W - win_c)
    row_ok = (r[None, :] >= r0[:, None]) & (r[None, :] < r0[:, None] + wr)
    col_ok = (c[None, :] >= c0[:, None]) & (c[None, :] < c0[:, None] + win_c)
    valid = (row_ok[:, None, :, None] & col_ok[None, :, None, :]).reshape(seq, seq)
    sel_r = (r[None, :, None] - r[:, None, None] + win_r - 1 == np.arange(2 * win_r - 1)).astype(np.float32)
    sel_c = (c[None, :, None] - c[:, None, None] + win_c - 1 == np.arange(2 * win_c - 1)).astype(np.float32)
    table = jnp.einsum("xka,lhab,cqb->lhxckq", sel_r, rpb.astype(F32), sel_c, precision=lax.Precision.HIGHEST)
    table = table.reshape(rpb.shape[0], rpb.shape[1], seq, seq)
    return jnp.where(valid[None, None], table, NEG_INF)


def _na_attn_body(zq, zk, zv, kc_ref, vc_ref, bias_ref, qg_ref, kg_ref, yb_ref, *, seq, dh, tq):
    scale = dh ** -0.5
    q = _head_rms(zq[...], qg_ref[...]).astype(BF16)
    k = _head_rms(zk[...], kg_ref[...]).astype(BF16)
    v = zv[...].astype(BF16)
    kc = kc_ref[0, 0, 0].astype(BF16)
    vc = vc_ref[0, 0, 0].astype(BF16)
    for i in range(seq // tq):
        rows = slice(i * tq, (i + 1) * tq)
        qt = q[rows]
        s_win = _bdot_nt(qt, k) * scale + bias_ref[0, rows, :]
        s_ctx = _bdot_nt(qt, kc) * scale
        m = jnp.maximum(jnp.max(s_win, axis=-1, keepdims=True), jnp.max(s_ctx, axis=-1, keepdims=True))
        p_win = jnp.exp(s_win - m)
        p_ctx = jnp.exp(s_ctx - m)
        den = jnp.sum(p_win, axis=-1, keepdims=True) + jnp.sum(p_ctx, axis=-1, keepdims=True)
        o = (_bdot(p_win, v) + _bdot(p_ctx, vc)) / den
        yb_ref[rows, :] = o.astype(yb_ref.dtype)


def _na_attention(z, cache_k, cache_v, bias, qg, kg, *, row_blk0, nb, seq, heads, dh, col0, layer):
    nw = heads * dh
    c0 = col0 // dh
    past = cache_k.shape[3]

    def zspec(k):
        return pl.BlockSpec((seq, dh), lambda h, b, k=k: (row_blk0 + b, c0 + k * heads + h))

    cache_spec = pl.BlockSpec((1, 1, 1, past, dh), lambda h, b: (b, layer, h, 0, 0))
    return pl.pallas_call(
        functools.partial(_na_attn_body, seq=seq, dh=dh, tq=min(256, seq)),
        grid=(heads, nb),
        in_specs=[zspec(0), zspec(1), zspec(2), cache_spec, cache_spec,
                  pl.BlockSpec((1, seq, seq), lambda h, b: (h, 0, 0)),
                  pl.BlockSpec((1, dh), lambda h, b: (0, 0)), pl.BlockSpec((1, dh), lambda h, b: (0, 0))],
        out_specs=pl.BlockSpec((seq, dh), lambda h, b: (b, h)),
        out_shape=jax.ShapeDtypeStruct((nb * seq, nw), BF16),
        compiler_params=_cparams(("parallel", "parallel")),
        name="na_attention",
    )(z, z, z, cache_k, cache_v, bias, qg.reshape(1, dh), kg.reshape(1, dh))


def _merge_body(yap_ref, yas_ref, ybp_ref, ybs_ref, wh_ref, wn_ref, ga_ref, gb_ref, o_ref, *, npt):
    is_prompt = pl.program_id(0) < npt
    ya = jnp.where(is_prompt, yap_ref[...], yas_ref[...])
    yb = jnp.where(is_prompt, ybp_ref[...], ybs_ref[...])
    a = jnp.dot(ya, wh_ref[...], preferred_element_type=F32)
    b = jnp.dot(yb, wn_ref[...], preferred_element_type=F32)
    o_ref[...] = (jax.nn.sigmoid(ga_ref[...]) * a + jax.nn.sigmoid(gb_ref[...]) * b).astype(o_ref.dtype)


def _merge(ya, yb, w_hb, w_nb, z, col_ga, n_prompt, ts):
    n = z.shape[0]
    hw, nw = ya[0].shape[1], yb[0].shape[1]
    d = w_hb.shape[1]
    tm, tn = min(512, ts), _pick(d, 1024)
    npt = n_prompt // tm
    ca, cbk = col_ga // tn, (col_ga + d) // tn

    def split(width):
        return [pl.BlockSpec((tm, width), lambda i, j: (jnp.minimum(i, npt - 1), 0)),
                pl.BlockSpec((tm, width), lambda i, j: (jnp.maximum(i - npt, 0), 0))]

    return pl.pallas_call(
        functools.partial(_merge_body, npt=npt),
        grid=(n // tm, d // tn),
        in_specs=split(hw) + split(nw) + [
            pl.BlockSpec((hw, tn), lambda i, j: (0, j)),
            pl.BlockSpec((nw, tn), lambda i, j: (0, j)),
            pl.BlockSpec((tm, tn), lambda i, j: (i, ca + j)),
            pl.BlockSpec((tm, tn), lambda i, j: (i, cbk + j)),
        ],
        out_specs=pl.BlockSpec((tm, tn), lambda i, j: (i, j)),
        out_shape=jax.ShapeDtypeStruct((n, d), BF16),
        compiler_params=_cparams(("parallel", "parallel")),
        name="merge",
    )(*ya, *yb, w_hb, w_nb, z, z)


def _outproj_body(*refs, n_experts, n_stream, npt):
    y = _stream_tile(refs[:n_stream], npt)
    m_ref, w_ref, mod_ref, g_ref = refs[n_stream:n_stream + 4]
    if n_experts:
        r_ref, y1_ref, h2_ref, gw_ref, gi_ref = refs[n_stream + 4:]
    else:
        y1_ref, h2_ref = refs[n_stream + 4:]
    y1 = y + mod_ref[0, 2:3, :] * jnp.dot(m_ref[...], w_ref[...], preferred_element_type=F32)
    y1_ref[...] = y1
    h2 = _rms_modulate(y1, g_ref[...], mod_ref[0, 3:4, :], mod_ref[0, 4:5, :])
    h2_ref[...] = h2.astype(h2_ref.dtype)
    if n_experts:
        r = r_ref[...]
        h_hi = h2.astype(BF16)
        h_lo = (h2 - h_hi.astype(F32)).astype(BF16)
        r_hi = r.astype(BF16)
        r_lo = (r - r_hi.astype(F32)).astype(BF16)
        logits = (jnp.dot(h_hi, r_hi, preferred_element_type=F32)
                  + jnp.dot(h_hi, r_lo, preferred_element_type=F32)
                  + jnp.dot(h_lo, r_hi, preferred_element_type=F32))
        lane = lax.broadcasted_iota(jnp.int32, logits.shape, 1).astype(F32)
        big = float(LANES)
        lg = jnp.where(lane < n_experts, logits, -jnp.inf)
        m1 = jnp.max(lg, axis=-1, keepdims=True)
        i1 = jnp.min(jnp.where(lg == m1, lane, big), axis=-1, keepdims=True)
        lg2 = jnp.where(lane == i1, -jnp.inf, lg)
        m2 = jnp.max(lg2, axis=-1, keepdims=True)
        i2 = jnp.min(jnp.where(lg2 == m2, lane, big), axis=-1, keepdims=True)
        t = jnp.exp(m2 - m1)
        w1 = 1.0 / (1.0 + t)
        w2 = t / (1.0 + t)
        gw_ref[...] = jnp.where(lane == 0.0, w1, jnp.where(lane == 1.0, w2, 0.0))
        gi_ref[...] = jnp.where(lane == 0.0, i1, jnp.where(lane == 1.0, i2, 0.0)).astype(jnp.int32)


def _outproj(m, w_out, y, mod_l, g2, n_prompt, ts, router=None):
    n, d = m.shape
    tm = min(256, ts)
    n_experts = 0 if router is None else router.shape[1]
    y_specs, y_args = _stream_specs(y, tm, n_prompt)
    in_specs = y_specs + [
        pl.BlockSpec((tm, d), lambda j: (j, 0)),
        pl.BlockSpec((d, d), lambda j: (0, 0)),
        pl.BlockSpec((1, 6, d), lambda j: (_cond_of_rows(j * tm, n_prompt, ts), 0, 0)),
        pl.BlockSpec((1, d), lambda j: (0, 0)),
    ]
    args = y_args + [m, w_out, mod_l, g2.reshape(1, d)]
    row_spec = pl.BlockSpec((tm, d), lambda j: (j, 0))
    out_specs = [row_spec, row_spec]
    out_shape = [jax.ShapeDtypeStruct((n, d), F32), jax.ShapeDtypeStruct((n, d), F32 if n_experts else BF16)]
    if n_experts:
        in_specs.append(pl.BlockSpec((d, LANES), lambda j: (0, 0)))
        args.append(jnp.pad(router, ((0, 0), (0, LANES - n_experts))))
        lane_spec = pl.BlockSpec((tm, LANES), lambda j: (j, 0))
        out_specs += [lane_spec, lane_spec]
        out_shape += [jax.ShapeDtypeStruct((n, LANES), F32), jax.ShapeDtypeStruct((n, LANES), jnp.int32)]
    return pl.pallas_call(
        functools.partial(_outproj_body, n_experts=n_experts, n_stream=len(y_args), npt=n_prompt // tm),
        grid=(n // tm,),
        in_specs=in_specs,
        out_specs=out_specs,
        out_shape=out_shape,
        compiler_params=_cparams(("parallel",)),
        name="outproj",
    )(*args)


def _ffn_body(te_ref, nt_ref, *refs, residual):
    del te_ref
    if residual:
        x_ref, wg_ref, wu_ref, wd_ref, y_ref, mod_ref, o_ref = refs
    else:
        x_ref, wg_ref, wu_ref, wd_ref, o_ref = refs
    j, f = pl.program_id(0), pl.program_id(1)
    nf = pl.num_programs(1)

    @pl.when(j < nt_ref[0])
    def _():
        x = x_ref[...].astype(BF16)
        g = jnp.dot(x, wg_ref[0], preferred_element_type=F32)
        u = jnp.dot(x, wu_ref[0], preferred_element_type=F32)
        part = jnp.dot((_silu(g) * u).astype(BF16), wd_ref[0], preferred_element_type=F32)

        @pl.when(f == 0)
        def _():
            o_ref[...] = part

        @pl.when(f > 0)
        def _():
            o_ref[...] += part

        if residual:
            @pl.when(f == nf - 1)
            def _():
                o_ref[...] = y_ref[...] + mod_ref[0, 5:6, :] * o_ref[...]

    @pl.when((j >= nt_ref[0]) & (f == 0))
    def _():
        o_ref[...] = jnp.zeros_like(o_ref)


def _ffn(x, wg, wu, wd, tile_expert, n_tiles, *, tm, tf, residual=None):
    n, d = x.shape
    ff = wg.shape[2]
    tf = _pick(ff, tf)
    nf = ff // tf

    def row_map(j, f, te, nt):
        return (jnp.minimum(j, nt[0] - 1), 0)

    def ff_idx(j, f, nt):
        return jnp.where(j < nt[0], f, nf - 1)

    def e_idx(j, te, nt):
        return te[jnp.minimum(j, nt[0] - 1)]

    in_specs = [
        pl.BlockSpec((tm, d), row_map),
        pl.BlockSpec((1, d, tf), lambda j, f, te, nt: (e_idx(j, te, nt), 0, ff_idx(j, f, nt))),
        pl.BlockSpec((1, d, tf), lambda j, f, te, nt: (e_idx(j, te, nt), 0, ff_idx(j, f, nt))),
        pl.BlockSpec((1, tf, d), lambda j, f, te, nt: (e_idx(j, te, nt), ff_idx(j, f, nt), 0)),
    ]
    args = [x, wg, wu, wd]
    if residual is not None:
        y, mod_l, n_prompt, ts = residual
        in_specs += [
            pl.BlockSpec((tm, d), row_map),
            pl.BlockSpec((1, 6, d), lambda j, f, te, nt: (_cond_of_rows(j * tm, n_prompt, ts), 0, 0)),
        ]
        args += [y, mod_l]
    return pl.pallas_call(
        functools.partial(_ffn_body, residual=residual is not None),
        grid_spec=pltpu.PrefetchScalarGridSpec(
            num_scalar_prefetch=2,
            grid=(n // tm, nf),
            in_specs=in_specs,
            out_specs=pl.BlockSpec((tm, d), lambda j, f, te, nt: (j, 0)),
        ),
        out_shape=jax.ShapeDtypeStruct((n, d), F32),
        compiler_params=_cparams(("arbitrary", "arbitrary")),
        name="swiglu",
    )(tile_expert, n_tiles, *args)


def _dispatch_body(fill_ref, p0_ref, p1_ref, src, dst, zero_s, sem, zsem, *, tb, tm):
    @pl.when(pl.program_id(0) == 0)
    def _():
        zero_s[...] = jnp.zeros_like(zero_s)

        def fill(u):
            row0 = pl.multiple_of(jnp.maximum(fill_ref[u], 0) * tm, tm)
            return pltpu.make_async_copy(zero_s, dst.at[pl.ds(row0, tm)], zsem)

        for u in range(fill_ref.shape[0]):
            pl.when(fill_ref[u] >= 0)(fill(u).start)
        for u in range(fill_ref.shape[0]):
            pl.when(fill_ref[u] >= 0)(fill(u).wait)

    def issue(t, c):
        row = src.at[pl.ds(t, 1)]
        pltpu.make_async_copy(row, dst.at[pl.ds(p0_ref[0, 0, t], 1)], sem).start()
        pltpu.make_async_copy(row, dst.at[pl.ds(p1_ref[0, 0, t], 1)], sem).start()
        return c

    lax.fori_loop(0, tb, issue, 0)
    for _ in range(2):
        pltpu.make_async_copy(src, dst.at[pl.ds(0, tb)], sem).wait()


def _dispatch(h, pos0, pos1, fill_tiles, n_sorted, tb, tm):
    n, d = h.shape
    return pl.pallas_call(
        functools.partial(_dispatch_body, tb=tb, tm=tm),
        grid=(n // tb,),
        in_specs=[
            pl.BlockSpec(memory_space=pltpu.SMEM),
            pl.BlockSpec((1, 1, tb), lambda i: (i, 0, 0), memory_space=pltpu.SMEM),
            pl.BlockSpec((1, 1, tb), lambda i: (i, 0, 0), memory_space=pltpu.SMEM),
            pl.BlockSpec((tb, d), lambda i: (i, 0)),
        ],
        out_specs=pl.BlockSpec(memory_space=pl.ANY),
        out_shape=jax.ShapeDtypeStruct((n_sorted, d), h.dtype),
        scratch_shapes=[pltpu.VMEM((tm, d), h.dtype), pltpu.SemaphoreType.DMA(()), pltpu.SemaphoreType.DMA(())],
        compiler_params=_cparams(("arbitrary",)),
        name="dispatch",
    )(fill_tiles, pos0.reshape(n // tb, 1, tb), pos1.reshape(n // tb, 1, tb), h)


def _combine_body(p0_ref, p1_ref, o_hbm, y_ref, gw_ref, mod_ref, outp_ref, outs_ref, buf0, buf1, sem, *, tc, npt):
    def issue(t, c):
        pltpu.make_async_copy(o_hbm.at[pl.ds(p0_ref[0, 0, t], 1)], buf0.at[pl.ds(t, 1)], sem).start()
        pltpu.make_async_copy(o_hbm.at[pl.ds(p1_ref[0, 0, t], 1)], buf1.at[pl.ds(t, 1)], sem).start()
        return c

    lax.fori_loop(0, tc, issue, 0)
    pltpu.make_async_copy(o_hbm.at[pl.ds(0, tc)], buf0, sem).wait()
    pltpu.make_async_copy(o_hbm.at[pl.ds(0, tc)], buf1, sem).wait()
    gw = gw_ref[...]
    mix = gw[:, 0:1] * buf0[...] + gw[:, 1:2] * buf1[...]
    res = y_ref[...] + mod_ref[0, 5:6, :] * mix
    i = pl.program_id(0)

    @pl.when(i < npt)
    def _():
        outp_ref[...] = res

    @pl.when(i >= npt)
    def _():
        outs_ref[...] = res


def _combine(o_sorted, pos0, pos1, y, gw, mod_l, n_prompt, ts):
    n, d = y.shape
    tc = min(256, ts)
    npt = n_prompt // tc
    pos0, pos1 = pos0.reshape(n // tc, 1, tc), pos1.reshape(n // tc, 1, tc)
    return pl.pallas_call(
        functools.partial(_combine_body, tc=tc, npt=npt),
        grid=(n // tc,),
        in_specs=[
            pl.BlockSpec((1, 1, tc), lambda i: (i, 0, 0), memory_space=pltpu.SMEM),
            pl.BlockSpec((1, 1, tc), lambda i: (i, 0, 0), memory_space=pltpu.SMEM),
            pl.BlockSpec(memory_space=pl.ANY),
            pl.BlockSpec((tc, d), lambda i: (i, 0)),
            pl.BlockSpec((tc, LANES), lambda i: (i, 0)),
            pl.BlockSpec((1, 6, d), lambda i: (_cond_of_rows(i * tc, n_prompt, ts), 0, 0)),
        ],
        out_specs=[pl.BlockSpec((tc, d), lambda i: (jnp.minimum(i, npt - 1), 0)),
                   pl.BlockSpec((tc, d), lambda i: (jnp.maximum(i - npt, 0), 0))],
        out_shape=[jax.ShapeDtypeStruct((n_prompt, d), F32), jax.ShapeDtypeStruct((n - n_prompt, d), F32)],
        scratch_shapes=[pltpu.VMEM((tc, d), F32), pltpu.VMEM((tc, d), F32), pltpu.SemaphoreType.DMA(())],
        compiler_params=_cparams(("arbitrary",)),
        name="combine",
    )(pos0, pos1, o_sorted, y, gw, mod_l)


def _routing_tables(top_i, n_experts, tm, n_tiles_max):
    n = top_i.shape[0]
    flat = jnp.concatenate([top_i[:, 0], top_i[:, 1]])
    onehot = (flat[:, None] == jnp.arange(n_experts, dtype=jnp.int32)[None, :]).astype(jnp.int32)
    csum = jnp.cumsum(onehot, axis=0)
    rank = jnp.sum((csum - onehot) * onehot, axis=1)
    counts = csum[-1]
    tiles = (counts + tm - 1) // tm
    tile_end = jnp.cumsum(tiles)
    start = (tile_end - tiles) * tm
    pos = jnp.sum(onehot * start[None, :], axis=1) + rank
    n_tiles = tile_end[-1]
    j = jnp.arange(n_tiles_max, dtype=jnp.int32)
    tile_expert = jnp.sum((j[:, None] >= tile_end[None, :]).astype(jnp.int32), axis=1)
    tile_expert = jnp.minimum(tile_expert, n_experts - 1).astype(jnp.int32)
    group_last = jnp.where(tiles > 0, tile_end - 1, -1)
    tail = n_tiles + jnp.arange(n_experts, dtype=jnp.int32)
    tail = jnp.where(tail < n_tiles_max, tail, -1)
    fill_tiles = jnp.concatenate([group_last, tail]).astype(jnp.int32)
    return (pos[:n].astype(jnp.int32), pos[n:].astype(jnp.int32), tile_expert,
            n_tiles.reshape(1).astype(jnp.int32), fill_tiles)


def kernel(x_prompt, x_sample, cache_k, cache_v, state_hgrn, c, c_ctx, norm1_g, norm2_g, w_ada, b_ada, w_in,
           hg_lb, hg_onorm_g, na_qn_g, na_kn_g, na_rpb, w_hb, w_nb, w_out, ffn_wg, ffn_wu, ffn_wd,
           moe_router, moe_wg, moe_wu, moe_wd):
    bp, tp, d = x_prompt.shape
    bs, ts, _ = x_sample.shape
    depth = w_in.shape[0]
    hg_heads, dh = state_hgrn.shape[3], state_hgrn.shape[4]
    na_heads = cache_k.shape[2]
    hw, nw = hg_heads * dh, na_heads * dh
    n_experts = moe_wg.shape[1]
    n_prompt, n_sample = bp * tp, bs * ts
    n = n_prompt + n_sample
    assert n_prompt % ts == 0 and ts % tp == 0 and bs + 1 <= N_COND
    col_nq = 5 * hw
    col_ga = 5 * hw + 3 * nw

    p = jax.nn.softmax(hg_lb.astype(F32), axis=1)
    cs = jnp.cumsum(p, axis=1)
    lbs = cs - cs[:, :1]
    bias = _na_bias(na_rpb, ts)
    w_in_b, w_hb_b, w_nb_b, w_out_b = (a.astype(BF16) for a in (w_in, w_hb, w_nb, w_out))
    ffn_b = tuple(a.astype(BF16) for a in (ffn_wg, ffn_wu, ffn_wd))
    moe_b = tuple(a.astype(BF16) for a in (moe_wg, moe_wu, moe_wd))

    cond = jnp.concatenate([c_ctx[None, :], c, jnp.zeros((N_COND - 1 - bs, d), F32)], axis=0)
    mod = _modulation(cond, w_ada, b_ada).reshape(depth, N_COND, 6, d)

    y = (x_prompt.reshape(n_prompt, d), x_sample.reshape(n_sample, d))
    tm_moe = min(512, n)
    n_tiles_max = (2 * n) // tm_moe + n_experts
    ks_out, vs_out, ss_out = [], [], []
    for l in range(depth):
        h = _prenorm(y, norm1_g[l], mod[l], n, n_prompt, ts)
        z = _matmul(h, w_in_b[l])
        ya_p, s_ctx = _hgrn(z, lbs[:, l], hg_onorm_g[l], row_blk0=0, nb=bp, seq=tp, heads=hg_heads, dh=dh,
                            layer=l, want_state=True)
        ya_s = _hgrn(z, lbs[:, l], hg_onorm_g[l], row_blk0=n_prompt // ts, nb=bs, seq=ts, heads=hg_heads,
                     dh=dh, layer=l, s0=state_hgrn)
        yb_p, k_new, v_new = _ctx_attention(z, na_qn_g[l], na_kn_g[l], nb=bp, seq=tp, heads=na_heads, dh=dh,
                                            col0=col_nq)
        yb_s = _na_attention(z, cache_k, cache_v, bias[l], na_qn_g[l], na_kn_g[l], row_blk0=n_prompt // ts,
                             nb=bs, seq=ts, heads=na_heads, dh=dh, col0=col_nq, layer=l)
        ks_out.append(k_new)
        vs_out.append(v_new)
        ss_out.append(s_ctx)
        m = _merge((ya_p, ya_s), (yb_p, yb_s), w_hb_b[l], w_nb_b[l], z, col_ga, n_prompt, ts)
        i = l // 2
        if l % 2 == 0:
            y1, h2 = _outproj(m, w_out_b[l], y, mod[l], norm2_g[l], n_prompt, ts)
            tm = min(512, ts)
            y = _ffn(h2, ffn_b[0][i][None], ffn_b[1][i][None], ffn_b[2][i][None],
                     jnp.zeros((n // tm,), jnp.int32), jnp.full((1,), n // tm, jnp.int32), tm=tm, tf=512,
                     residual=(y1, mod[l], n_prompt, ts))
        else:
            y1, h2, gw, gi = _outproj(m, w_out_b[l], y, mod[l], norm2_g[l], n_prompt, ts, router=moe_router[i])
            pos0, pos1, tile_expert, n_tiles, fill_tiles = _routing_tables(gi[:, :2], n_experts, tm_moe,
                                                                          n_tiles_max)
            xs = _dispatch(h2, pos0, pos1, fill_tiles, n_tiles_max * tm_moe, min(1024, ts), tm_moe)
            o_sorted = _ffn(xs, moe_b[0][i], moe_b[1][i], moe_b[2][i], tile_expert, n_tiles, tm=tm_moe, tf=1024)
            y = tuple(_combine(o_sorted, pos0, pos1, y1, gw, mod[l], n_prompt, ts))

    if not isinstance(y, tuple):
        y = (y[:n_prompt], y[n_prompt:])
    new_cache_k = jnp.concatenate(ks_out, axis=1)
    new_cache_v = jnp.concatenate(vs_out, axis=1)
    new_state_hgrn = jnp.concatenate(ss_out, axis=1)
    return (y[0].reshape(bp, tp, d), y[1].reshape(bs, ts, d), new_cache_k, new_cache_v, new_state_hgrn)
```

```python
import functools

import numpy as np
import jax
import jax.numpy as jnp
from jax import lax
from jax.experimental import pallas as pl
from jax.experimental.pallas import tpu as pltpu

F32 = jnp.float32
BF16 = jnp.bfloat16

NORM_EPS = 1e-6
NEG_INF = -1e30
GRID_W = 64
NA_QB = 16
LANES = 128
N_COND = 16
HG_BLOCK = 32
HG_HALF = HG_BLOCK // 2
VMEM_LIMIT = 56 * 1024 * 1024


def _cparams(semantics, vmem=VMEM_LIMIT):
    return pltpu.CompilerParams(dimension_semantics=semantics, vmem_limit_bytes=vmem)


def _silu(x):
    return x * jax.nn.sigmoid(x)


def _bdot(a, b):
    return jnp.dot(a.astype(BF16), b.astype(BF16), preferred_element_type=F32)


def _bdot_nt(a, b):
    return lax.dot_general(a.astype(BF16), b.astype(BF16), (((1,), (1,)), ((), ())),
                           preferred_element_type=F32)


def _bdot_tn(a, b):
    return lax.dot_general(a.astype(BF16), b.astype(BF16), (((0,), (0,)), ((), ())),
                           preferred_element_type=F32)


def _pick(n, pref):
    if n <= pref:
        return n
    t = pref - pref % LANES
    while n % t:
        t -= LANES
    return t


def _cond_of_rows(row0, n_prompt, ts):
    return jnp.where(row0 < n_prompt, 0, 1 + (row0 - n_prompt) // ts)


def _mod_body(c_ref, w_ref, b_ref, o_ref):
    o_ref[0] = _bdot(_silu(c_ref[...]), w_ref[0]) + b_ref[0]


def _modulation(cond, w_ada, b_ada):
    depth, d, n6 = w_ada.shape
    tn = _pick(n6, 1024)
    return pl.pallas_call(
        _mod_body,
        grid=(depth, n6 // tn),
        in_specs=[
            pl.BlockSpec((N_COND, d), lambda l, j: (0, 0)),
            pl.BlockSpec((1, d, tn), lambda l, j: (l, 0, j)),
            pl.BlockSpec((1, 1, tn), lambda l, j: (l, 0, j)),
        ],
        out_specs=pl.BlockSpec((1, N_COND, tn), lambda l, j: (l, 0, j)),
        out_shape=jax.ShapeDtypeStruct((depth, N_COND, n6), F32),
        compiler_params=_cparams(("parallel", "parallel")),
        name="modulation",
    )(cond, w_ada, b_ada.reshape(depth, 1, n6))


def _rms_modulate(x, g, shift, scale):
    ms = jnp.mean(x * x, axis=-1, keepdims=True)
    return (x * lax.rsqrt(ms + NORM_EPS) * g) * (1.0 + scale) + shift


def _stream_specs(y, tm, n_prompt):
    if not isinstance(y, tuple):
        return [pl.BlockSpec((tm, y.shape[1]), lambda j: (j, 0))], [y]
    d = y[0].shape[1]
    npt = n_prompt // tm
    return ([pl.BlockSpec((tm, d), lambda j: (jnp.minimum(j, npt - 1), 0)),
             pl.BlockSpec((tm, d), lambda j: (jnp.maximum(j - npt, 0), 0))], list(y))


def _stream_tile(refs, npt):
    if len(refs) == 1:
        return refs[0][...]
    return jnp.where(pl.program_id(0) < npt, refs[0][...], refs[1][...])


def _prenorm_body(*refs, npt):
    g_ref, m_ref, o_ref = refs[-3:]
    y = _stream_tile(refs[:-3], npt)
    o_ref[...] = _rms_modulate(y, g_ref[...], m_ref[0, 0:1, :], m_ref[0, 1:2, :]).astype(o_ref.dtype)


def _prenorm(y, g, mod_l, n, n_prompt, ts):
    d = g.shape[0]
    tm = min(512, ts)
    y_specs, y_args = _stream_specs(y, tm, n_prompt)
    return pl.pallas_call(
        functools.partial(_prenorm_body, npt=n_prompt // tm),
        grid=(n // tm,),
        in_specs=y_specs + [
            pl.BlockSpec((1, d), lambda j: (0, 0)),
            pl.BlockSpec((1, 6, d), lambda j: (_cond_of_rows(j * tm, n_prompt, ts), 0, 0)),
        ],
        out_specs=pl.BlockSpec((tm, d), lambda j: (j, 0)),
        out_shape=jax.ShapeDtypeStruct((n, d), BF16),
        compiler_params=_cparams(("parallel",)),
        name="prenorm",
    )(*y_args, g.reshape(1, d), mod_l)


def _rider_specs(riders, n_steps, step_of):
    in_specs, out_specs, out_shape = [], [], []
    for r in riders:
        rows, cols = r.shape
        assert rows % (n_steps * 16) == 0, (r.shape, n_steps)
        blk = (rows // n_steps, cols)
        in_specs.append(pl.BlockSpec(blk, lambda *g: (step_of(*g), 0)))
        out_specs.append(pl.BlockSpec(blk, lambda *g: (step_of(*g), 0)))
        out_shape.append(jax.ShapeDtypeStruct(r.shape, BF16))
    return in_specs, out_specs, out_shape


def _mm_body(x_ref, w_ref, *refs):
    n_r = (len(refs) - 1) // 2
    o_ref = refs[n_r]
    o_ref[...] = jnp.dot(x_ref[...], w_ref[...], preferred_element_type=F32).astype(o_ref.dtype)
    for src, dst in zip(refs[:n_r], refs[n_r + 1:]):
        dst[...] = src[...].astype(dst.dtype)


def _matmul(x, w, riders=()):
    m, k = x.shape
    n = w.shape[1]
    tm, tn = _pick(m, 1024), _pick(n, 1536)
    gm, gn = m // tm, n // tn
    r_in, r_out, r_shape = _rider_specs(riders, gm * gn, lambda i, j: i * gn + j)
    res = pl.pallas_call(
        _mm_body,
        grid=(gm, gn),
        in_specs=[pl.BlockSpec((tm, k), lambda i, j: (i, 0)), pl.BlockSpec((k, tn), lambda i, j: (0, j))] + r_in,
        out_specs=[pl.BlockSpec((tm, tn), lambda i, j: (i, j))] + r_out,
        out_shape=[jax.ShapeDtypeStruct((m, n), F32)] + r_shape,
        compiler_params=_cparams(("parallel", "parallel")),
        name="in_proj",
    )(x, w, *riders)
    return res[0], res[1:]


def _hgrn_gates(z, lb):
    t = jnp.exp(-jnp.abs(z))
    pos = z >= 0.0
    inv = 1.0 / (1.0 + t)
    f = jnp.where(pos, 1.0 + lb * t, lb + t) * inv
    k = (1.0 - lb) * jnp.where(pos, t, 1.0) * inv
    return jnp.log(f), k


def _split2(x):
    hi = x.astype(BF16)
    lo = (x - hi.astype(F32)).astype(BF16)
    return hi, lo


def _hgrn_body(*refs, seq, hp, dh, has_s0, has_sfin, unroll):
    zq, zff, zfb, zi, zg, lb_ref, og_ref = refs[:7]
    pos = 7
    s0_ref = None
    if has_s0:
        s0_ref = refs[pos]
        pos += 1
    ya_ref = refs[pos]
    pos += 1
    sfin_ref = None
    if has_sfin:
        sfin_ref = refs[pos]
        pos += 1
    o_s, st_s = refs[pos:pos + 2]
    w = hp * dh
    nblk = seq // HG_BLOCK
    zf = (zff, zfb)

    for d in range(2):
        for h in range(hp):
            if has_s0:
                st_s[d * hp + h] = s0_ref[0, 0, d, h].T
            else:
                st_s[d * hp + h] = jnp.zeros((dh, dh), F32)

    row = lax.broadcasted_iota(jnp.int32, (HG_BLOCK, HG_BLOCK), 0)
    col = lax.broadcasted_iota(jnp.int32, (HG_BLOCK, HG_BLOCK), 1)
    same_half = (row // HG_HALF) == (col // HG_HALF)
    rowv = lax.broadcasted_iota(jnp.int32, (HG_BLOCK, 1), 0)
    attend = (col <= row, col >= row)
    seg = tuple(jnp.where(a & same_half, 1.0, 0.0).astype(BF16) for a in attend)
    first_half = (rowv < HG_HALF, rowv >= HG_HALF)
    first_end = (HG_HALF - 1, HG_HALF)
    second_end = (HG_BLOCK - 1, 0)

    units = [(d, h) for d in range(2) for h in range(hp)]

    def block(i, carry):
        rows = (pl.ds(pl.multiple_of(i * HG_BLOCK, HG_BLOCK), HG_BLOCK),
                pl.ds(pl.multiple_of(seq - HG_BLOCK * (i + 1), HG_BLOCK), HG_BLOCK))
        cum2, kbs = [], []
        for d in range(2):
            lf, kb = _hgrn_gates(zf[d][rows[d], :], lb_ref[d:d + 1, :])
            kbs.append(kb)
            hi, lo = _split2(lf)
            cum2.append(jnp.dot(seg[d], jnp.concatenate([hi, lo], axis=1), preferred_element_type=F32))
        vb = [zi[rows[d], :].astype(BF16) for d in range(2)]
        qe, ke, qs, kend, dec = [], [], [], [], []
        for d in range(2):
            qb = _silu(zq[rows[d], :])
            kb = kbs[d]
            cum = cum2[d][:, :w] + cum2[d][:, w:]
            l_first = cum[first_end[d]:first_end[d] + 1, :]
            l_second = cum[second_end[d]:second_end[d] + 1, :]
            l_blk = l_first + l_second
            cum_blk = cum + jnp.where(first_half[d], 0.0, l_first)
            e = cum_blk - l_first
            qe.append((qb * jnp.exp(e)).astype(BF16))
            ke.append((kb * jnp.exp(-e)).astype(BF16))
            qs.append((qb * jnp.exp(cum_blk)).astype(BF16))
            kend.append((kb * jnp.exp(l_blk - cum_blk)).astype(BF16))
            dec.append(jnp.exp(l_blk))
        sls = [slice(h * dh, (h + 1) * dh) for h in range(hp)]
        a = [_bdot_nt(qe[d][:, sls[h]], ke[d][:, sls[h]]) for d, h in units]
        st = [st_s[d * hp + h] for d, h in units]
        inter = [_bdot_nt(qs[d][:, sls[h]], st[u]) for u, (d, h) in enumerate(units)]
        ds = [_bdot_tn(vb[d][:, sls[h]], kend[d][:, sls[h]]) for d, h in units]
        intra = [_bdot(jnp.where(attend[d], a[u], 0.0), vb[d][:, sls[h]]) for u, (d, h) in enumerate(units)]
        for u, (d, h) in enumerate(units):
            st_s[d * hp + h] = st[u] * dec[d][:, sls[h]] + ds[u]
            o_s[d, rows[d], sls[h]] = intra[u] + inter[u]
        return carry

    lax.fori_loop(0, nblk, block, 0, unroll=unroll)

    def finish(c, carry):
        rows = pl.ds(pl.multiple_of(c * HG_BLOCK, HG_BLOCK), HG_BLOCK)
        o = o_s[0, rows, :] + o_s[1, rows, :]
        gate = _silu(zg[rows, :])
        for h in range(hp):
            sl = slice(h * dh, (h + 1) * dh)
            oh = o[:, sl]
            ms = jnp.mean(oh * oh, axis=-1, keepdims=True)
            ya_ref[rows, sl] = (oh * lax.rsqrt(ms + NORM_EPS) * og_ref[...] * gate[:, sl]).astype(ya_ref.dtype)
        return carry

    lax.fori_loop(0, nblk, finish, 0)
    if has_sfin:
        for d in range(2):
            for h in range(hp):
                sfin_ref[0, 0, d, h] = st_s[d * hp + h].T


def _hgrn(z, lb_l, og_l, *, row_blk0, nb, seq, heads, dh, layer, s0=None, want_state=False, hp=None, unroll=2):
    if hp is None:
        hp = next(c for c in (4, 2, 1) if heads % c == 0)
    w = hp * dh
    hw = heads * dh
    ng = heads // hp
    cb = hw // w

    def zspec(k):
        return pl.BlockSpec((seq, w), lambda b, g, k=k: (row_blk0 + b, k * cb + g))

    in_specs = [zspec(0), zspec(1), zspec(2), zspec(3), zspec(4),
                pl.BlockSpec((2, w), lambda b, g: (0, g)),
                pl.BlockSpec((1, dh), lambda b, g: (0, 0))]
    args = [z, z, z, z, z, lb_l, og_l.reshape(1, dh)]
    if s0 is not None:
        in_specs.append(pl.BlockSpec((1, 1, 2, hp, dh, dh), lambda b, g: (b, layer, 0, g, 0, 0)))
        args.append(s0)
    out_specs = [pl.BlockSpec((seq, w), lambda b, g: (b, g))]
    out_shape = [jax.ShapeDtypeStruct((nb * seq, hw), BF16)]
    if want_state:
        out_specs.append(pl.BlockSpec((1, 1, 2, hp, dh, dh), lambda b, g: (b, 0, 0, g, 0, 0)))
        out_shape.append(jax.ShapeDtypeStruct((nb, 1, 2, heads, dh, dh), F32))
    res = pl.pallas_call(
        functools.partial(_hgrn_body, seq=seq, hp=hp, dh=dh, has_s0=s0 is not None, has_sfin=want_state,
                          unroll=unroll),
        grid=(nb, ng),
        in_specs=in_specs,
        out_specs=out_specs,
        out_shape=out_shape,
        scratch_shapes=[
            pltpu.VMEM((2, seq, w), F32),
            pltpu.VMEM((2 * hp, dh, dh), F32),
        ],
        compiler_params=_cparams(("parallel", "parallel")),
        name="hgrn",
    )(*args)
    return res if want_state else res[0]


def _head_rms(x, g):
    ms = jnp.mean(x * x, axis=-1, keepdims=True)
    return x * lax.rsqrt(ms + NORM_EPS) * g


def _ctx_attn_body(zq, zk, zv, qg_ref, kg_ref, yb_ref, k_ref, v_ref, *, hp, dh):
    scale = dh ** -0.5
    for h in range(hp):
        sl = slice(h * dh, (h + 1) * dh)
        q = _head_rms(zq[:, sl], qg_ref[...])
        k = _head_rms(zk[:, sl], kg_ref[...])
        v = zv[:, sl]
        k_ref[0, 0, h] = k
        v_ref[0, 0, h] = v
        s = _bdot_nt(q, k) * scale
        p = jnp.exp(s - jnp.max(s, axis=-1, keepdims=True))
        o = _bdot(p, v) / jnp.sum(p, axis=-1, keepdims=True)
        yb_ref[:, sl] = o.astype(yb_ref.dtype)


def _ctx_attention(z, qg, kg, *, nb, seq, heads, dh, col0):
    hp = 2 if heads % 2 == 0 else 1
    w = hp * dh
    nw = heads * dh
    ng = heads // hp
    c0 = col0 // w
    cb = nw // w

    def zspec(k):
        return pl.BlockSpec((seq, w), lambda b, g, k=k: (b, c0 + k * cb + g))

    kv_spec = pl.BlockSpec((1, 1, hp, seq, dh), lambda b, g: (b, 0, g, 0, 0))
    kv_shape = jax.ShapeDtypeStruct((nb, 1, heads, seq, dh), F32)
    return pl.pallas_call(
        functools.partial(_ctx_attn_body, hp=hp, dh=dh),
        grid=(nb, ng),
        in_specs=[zspec(0), zspec(1), zspec(2),
                  pl.BlockSpec((1, dh), lambda b, g: (0, 0)), pl.BlockSpec((1, dh), lambda b, g: (0, 0))],
        out_specs=[pl.BlockSpec((seq, w), lambda b, g: (b, g)), kv_spec, kv_spec],
        out_shape=[jax.ShapeDtypeStruct((nb * seq, nw), BF16), kv_shape, kv_shape],
        compiler_params=_cparams(("parallel", "parallel")),
        name="ctx_attention",
    )(z, z, z, qg.reshape(1, dh), kg.reshape(1, dh))


def _na_bias(rpb, seq):
    win_r, win_c = (rpb.shape[2] + 1) // 2, (rpb.shape[3] + 1) // 2
    rows = seq // GRID_W
    wr = min(win_r, rows)
    r, c = np.arange(rows), np.arange(GRID_W)
    r0 = np.clip(r - wr // 2, 0, rows - wr)
    c0 = np.clip(c - win_c // 2, 0, GRID_W - win_c)
    row_ok = (r[None, :] >= r0[:, None]) & (r[None, :] < r0[:, None] + wr)
    col_ok = (c[None, :] >= c0[:, None]) & (c[None, :] < c0[:, None] + win_c)
    valid = (row_ok[:, None, :, None] & col_ok[None, :, None, :]).reshape(seq, seq)
    sel_r = (r[None, :, None] - r[:, None, None] + win_r - 1 == np.arange(2 * win_r - 1)).astype(np.float32)
    sel_c = (c[None, :, None] - c[:, None, None] + win_c - 1 == np.arange(2 * win_c - 1)).astype(np.float32)
    table = jnp.einsum("xka,lhab,cqb->lhxckq", sel_r, rpb.astype(F32), sel_c, precision=lax.Precision.HIGHEST)
    table = table.reshape(rpb.shape[0], rpb.shape[1], seq, seq)
    return jnp.where(valid[None, None], table, NEG_INF)


def _na_attn_body(zq, zk, zv, kc_ref, vc_ref, bias_ref, qg_ref, kg_ref, yb_ref, *, seq, dh, tq):
    scale = dh ** -0.5
    q = _head_rms(zq[...], qg_ref[...]).astype(BF16)
    k = _head_rms(zk[...], kg_ref[...]).astype(BF16)
    v = zv[...].astype(BF16)
    kc = kc_ref[0, 0, 0].astype(BF16)
    vc = vc_ref[0, 0, 0].astype(BF16)
    for i in range(seq // tq):
        rows = slice(i * tq, (i + 1) * tq)
        qt = q[rows]
        s_win = _bdot_nt(qt, k) * scale + bias_ref[0, rows, :]
        s_ctx = _bdot_nt(qt, kc) * scale
        m = jnp.maximum(jnp.max(s_win, axis=-1, keepdims=True), jnp.max(s_ctx, axis=-1, keepdims=True))
        p_win = jnp.exp(s_win - m)
        p_ctx = jnp.exp(s_ctx - m)
        den = jnp.sum(p_win, axis=-1, keepdims=True) + jnp.sum(p_ctx, axis=-1, keepdims=True)
        o = (_bdot(p_win, v) + _bdot(p_ctx, vc)) / den
        yb_ref[rows, :] = o.astype(yb_ref.dtype)


def _na_attention(z, cache_k, cache_v, bias, qg, kg, *, row_blk0, nb, seq, heads, dh, col0, layer):
    nw = heads * dh
    c0 = col0 // dh
    past = cache_k.shape[3]

    def zspec(k):
        return pl.BlockSpec((seq, dh), lambda h, b, k=k: (row_blk0 + b, c0 + k * heads + h))

    cache_spec = pl.BlockSpec((1, 1, 1, past, dh), lambda h, b: (b, layer, h, 0, 0))
    return pl.pallas_call(
        functools.partial(_na_attn_body, seq=seq, dh=dh, tq=min(256, seq)),
        grid=(heads, nb),
        in_specs=[zspec(0), zspec(1), zspec(2), cache_spec, cache_spec,
                  pl.BlockSpec((1, seq, seq), lambda h, b: (h, 0, 0)),
                  pl.BlockSpec((1, dh), lambda h, b: (0, 0)), pl.BlockSpec((1, dh), lambda h, b: (0, 0))],
        out_specs=pl.BlockSpec((seq, dh), lambda h, b: (b, h)),
        out_shape=jax.ShapeDtypeStruct((nb * seq, nw), BF16),
        compiler_params=_cparams(("parallel", "parallel")),
        name="na_attention",
    )(z, z, z, cache_k, cache_v, bias, qg.reshape(1, dh), kg.reshape(1, dh))


def _merge_body(yap_ref, yas_ref, ybp_ref, ybs_ref, wh_ref, wn_ref, ga_ref, gb_ref, o_ref, *, npt):
    is_prompt = pl.program_id(0) < npt
    ya = jnp.where(is_prompt, yap_ref[...], yas_ref[...])
    yb = jnp.where(is_prompt, ybp_ref[...], ybs_ref[...])
    a = jnp.dot(ya, wh_ref[...], preferred_element_type=F32)
    b = jnp.dot(yb, wn_ref[...], preferred_element_type=F32)
    o_ref[...] = (jax.nn.sigmoid(ga_ref[...]) * a + jax.nn.sigmoid(gb_ref[...]) * b).astype(o_ref.dtype)


def _merge(ya, yb, w_hb, w_nb, z, col_ga, n_prompt, ts):
    n = z.shape[0]
    hw, nw = ya[0].shape[1], yb[0].shape[1]
    d = w_hb.shape[1]
    tm, tn = min(512, ts), _pick(d, 1024)
    npt = n_prompt // tm
    ca, cbk = col_ga // tn, (col_ga + d) // tn

    def split(width):
        return [pl.BlockSpec((tm, width), lambda i, j: (jnp.minimum(i, npt - 1), 0)),
                pl.BlockSpec((tm, width), lambda i, j: (jnp.maximum(i - npt, 0), 0))]

    return pl.pallas_call(
        functools.partial(_merge_body, npt=npt),
        grid=(n // tm, d // tn),
        in_specs=split(hw) + split(nw) + [
            pl.BlockSpec((hw, tn), lambda i, j: (0, j)),
            pl.BlockSpec((nw, tn), lambda i, j: (0, j)),
            pl.BlockSpec((tm, tn), lambda i, j: (i, ca + j)),
            pl.BlockSpec((tm, tn), lambda i, j: (i, cbk + j)),
        ],
        out_specs=pl.BlockSpec((tm, tn), lambda i, j: (i, j)),
        out_shape=jax.ShapeDtypeStruct((n, d), BF16),
        compiler_params=_cparams(("parallel", "parallel")),
        name="merge",
    )(*ya, *yb, w_hb, w_nb, z, z)


def _outproj_body(*refs, n_experts, n_stream, npt):
    y = _stream_tile(refs[:n_stream], npt)
    m_ref, w_ref, mod_ref, g_ref = refs[n_stream:n_stream + 4]
    if n_experts:
        r_ref, y1_ref, h2_ref, gw_ref, gi_ref = refs[n_stream + 4:]
    else:
        y1_ref, h2_ref = refs[n_stream + 4:]
    y1 = y + mod_ref[0, 2:3, :] * jnp.dot(m_ref[...], w_ref[...], preferred_element_type=F32)
    y1_ref[...] = y1
    h2 = _rms_modulate(y1, g_ref[...], mod_ref[0, 3:4, :], mod_ref[0, 4:5, :])
    h2_ref[...] = h2.astype(h2_ref.dtype)
    if n_experts:
        r = r_ref[...]
        h_hi = h2.astype(BF16)
        h_lo = (h2 - h_hi.astype(F32)).astype(BF16)
        r_hi = r.astype(BF16)
        r_lo = (r - r_hi.astype(F32)).astype(BF16)
        logits = (jnp.dot(h_hi, r_hi, preferred_element_type=F32)
                  + jnp.dot(h_hi, r_lo, preferred_element_type=F32)
                  + jnp.dot(h_lo, r_hi, preferred_element_type=F32))
        lane = lax.broadcasted_iota(jnp.int32, logits.shape, 1).astype(F32)
        big = float(LANES)
        lg = jnp.where(lane < n_experts, logits, -jnp.inf)
        m1 = jnp.max(lg, axis=-1, keepdims=True)
        i1 = jnp.min(jnp.where(lg == m1, lane, big), axis=-1, keepdims=True)
        lg2 = jnp.where(lane == i1, -jnp.inf, lg)
        m2 = jnp.max(lg2, axis=-1, keepdims=True)
        i2 = jnp.min(jnp.where(lg2 == m2, lane, big), axis=-1, keepdims=True)
        t = jnp.exp(m2 - m1)
        w1 = 1.0 / (1.0 + t)
        w2 = t / (1.0 + t)
        gw_ref[...] = jnp.where(lane == 0.0, w1, jnp.where(lane == 1.0, w2, 0.0))
        gi_ref[...] = jnp.where(lane == 0.0, i1, jnp.where(lane == 1.0, i2, 0.0)).astype(jnp.int32)


def _outproj(m, w_out, y, mod_l, g2, n_prompt, ts, router=None):
    n, d = m.shape
    tm = min(256, ts)
    n_experts = 0 if router is None else router.shape[1]
    y_specs, y_args = _stream_specs(y, tm, n_prompt)
    in_specs = y_specs + [
        pl.BlockSpec((tm, d), lambda j: (j, 0)),
        pl.BlockSpec((d, d), lambda j: (0, 0)),
        pl.BlockSpec((1, 6, d), lambda j: (_cond_of_rows(j * tm, n_prompt, ts), 0, 0)),
        pl.BlockSpec((1, d), lambda j: (0, 0)),
    ]
    args = y_args + [m, w_out, mod_l, g2.reshape(1, d)]
    row_spec = pl.BlockSpec((tm, d), lambda j: (j, 0))
    out_specs = [row_spec, row_spec]
    out_shape = [jax.ShapeDtypeStruct((n, d), F32), jax.ShapeDtypeStruct((n, d), F32 if n_experts else BF16)]
    if n_experts:
        in_specs.append(pl.BlockSpec((d, LANES), lambda j: (0, 0)))
        args.append(jnp.pad(router, ((0, 0), (0, LANES - n_experts))))
        lane_spec = pl.BlockSpec((tm, LANES), lambda j: (j, 0))
        out_specs += [lane_spec, lane_spec]
        out_shape += [jax.ShapeDtypeStruct((n, LANES), F32), jax.ShapeDtypeStruct((n, LANES), jnp.int32)]
    return pl.pallas_call(
        functools.partial(_outproj_body, n_experts=n_experts, n_stream=len(y_args), npt=n_prompt // tm),
        grid=(n // tm,),
        in_specs=in_specs,
        out_specs=out_specs,
        out_shape=out_shape,
        compiler_params=_cparams(("parallel",)),
        name="outproj",
    )(*args)


def _ffn2_body(te_ref, nt_ref, *refs, residual, nf, nn, tf, n_riders):
    del te_ref
    x_ref, wg_ref, wu_ref, wd_ref = refs[:4]
    pos = 4
    if residual:
        y_ref, g2_ref = refs[4:6]
        pos = 6
    r_src = refs[pos:pos + n_riders]
    o_ref = refs[pos + n_riders]
    r_dst = refs[pos + n_riders + 1:pos + 2 * n_riders + 1]
    act_s = refs[-1]
    j, s = pl.program_id(0), pl.program_id(1)
    live = j < nt_ref[0]
    down = (s >= nf) & (s < nf + nn)
    for src, dst in zip(r_src, r_dst):
        dst[...] = src[...].astype(dst.dtype)

    @pl.when(live & (s < nf))
    def _():
        x = x_ref[...].astype(BF16)
        g = jnp.dot(x, wg_ref[0], preferred_element_type=F32)
        u = jnp.dot(x, wu_ref[0], preferred_element_type=F32)
        act_s[s] = (_silu(g) * u).astype(BF16)

    @pl.when(live & down)
    def _():
        r = jnp.dot(act_s[0], wd_ref[0, 0:tf, :], preferred_element_type=F32)
        for c in range(1, nf):
            r += jnp.dot(act_s[c], wd_ref[0, c * tf:(c + 1) * tf, :], preferred_element_type=F32)
        if residual:
            r = y_ref[...] + g2_ref[0] * r
        o_ref[...] = r

    @pl.when(jnp.logical_not(live) & down)
    def _():
        o_ref[...] = jnp.zeros_like(o_ref)


def _ffn2(x, wg, wu, wd, tile_expert, n_tiles, *, tm, tf, tn, residual=None, riders=()):
    n, d = x.shape
    ff = wg.shape[2]
    tf, tn = _pick(ff, tf), _pick(d, tn)
    nf, nn = ff // tf, d // tn
    gm, gs = n // tm, nf + nn
    while any(r.shape[0] % (gm * gs * 16) for r in riders):
        gs += 1

    def row(j, nt):
        return jnp.minimum(j, nt[0] - 1)

    def slab(s):
        return jnp.clip(s - nf, 0, nn - 1)

    def up_idx(j, s, te, nt):
        return (te[row(j, nt)], 0, jnp.where(j < nt[0], jnp.minimum(s, nf - 1), nf - 1))

    def down_idx(j, s, te, nt):
        return (te[row(j, nt)], 0, jnp.where(j < nt[0], slab(s), nn - 1))

    in_specs = [
        pl.BlockSpec((tm, d), lambda j, s, te, nt: (row(j, nt), 0)),
        pl.BlockSpec((1, d, tf), up_idx),
        pl.BlockSpec((1, d, tf), up_idx),
        pl.BlockSpec((1, ff, tn), down_idx),
    ]
    args = [x, wg, wu, wd]
    if residual is not None:
        y, gate2, n_prompt, ts = residual
        in_specs += [
            pl.BlockSpec((tm, tn), lambda j, s, te, nt: (row(j, nt), slab(s))),
            pl.BlockSpec((1, 1, tn), lambda j, s, te, nt: (_cond_of_rows(j * tm, n_prompt, ts), 0, slab(s))),
        ]
        args += [y, gate2]
    r_in, r_out, r_shape = _rider_specs(riders, gm * gs, lambda j, s, te, nt: j * gs + s)
    res = pl.pallas_call(
        functools.partial(_ffn2_body, residual=residual is not None, nf=nf, nn=nn, tf=tf, n_riders=len(riders)),
        grid_spec=pltpu.PrefetchScalarGridSpec(
            num_scalar_prefetch=2,
            grid=(gm, gs),
            in_specs=in_specs + r_in,
            out_specs=[pl.BlockSpec((tm, tn), lambda j, s, te, nt: (j, slab(s)))] + r_out,
            scratch_shapes=[pltpu.VMEM((nf, tm, tf), BF16)],
        ),
        out_shape=[jax.ShapeDtypeStruct((n, d), F32)] + r_shape,
        compiler_params=_cparams(("arbitrary", "arbitrary")),
        name="swiglu",
    )(tile_expert, n_tiles, *args, *riders)
    return res[0], res[1:]


def _dispatch_body(fill_ref, p0_ref, p1_ref, src, dst, zero_s, sem, zsem, *, tb, tm):
    @pl.when(pl.program_id(0) == 0)
    def _():
        zero_s[...] = jnp.zeros_like(zero_s)

        def fill(u):
            row0 = pl.multiple_of(jnp.maximum(fill_ref[u], 0) * tm, tm)
            return pltpu.make_async_copy(zero_s, dst.at[pl.ds(row0, tm)], zsem)

        for u in range(fill_ref.shape[0]):
            pl.when(fill_ref[u] >= 0)(fill(u).start)
        for u in range(fill_ref.shape[0]):
            pl.when(fill_ref[u] >= 0)(fill(u).wait)

    def issue(t, c):
        row = src.at[pl.ds(t, 1)]
        pltpu.make_async_copy(row, dst.at[pl.ds(p0_ref[0, 0, t], 1)], sem).start()
        pltpu.make_async_copy(row, dst.at[pl.ds(p1_ref[0, 0, t], 1)], sem).start()
        return c

    lax.fori_loop(0, tb, issue, 0)
    for _ in range(2):
        pltpu.make_async_copy(src, dst.at[pl.ds(0, tb)], sem).wait()


def _dispatch(h, pos0, pos1, fill_tiles, n_sorted, tb, tm):
    n, d = h.shape
    return pl.pallas_call(
        functools.partial(_dispatch_body, tb=tb, tm=tm),
        grid=(n // tb,),
        in_specs=[
            pl.BlockSpec(memory_space=pltpu.SMEM),
            pl.BlockSpec((1, 1, tb), lambda i: (i, 0, 0), memory_space=pltpu.SMEM),
            pl.BlockSpec((1, 1, tb), lambda i: (i, 0, 0), memory_space=pltpu.SMEM),
            pl.BlockSpec((tb, d), lambda i: (i, 0)),
        ],
        out_specs=pl.BlockSpec(memory_space=pl.ANY),
        out_shape=jax.ShapeDtypeStruct((n_sorted, d), h.dtype),
        scratch_shapes=[pltpu.VMEM((tm, d), h.dtype), pltpu.SemaphoreType.DMA(()), pltpu.SemaphoreType.DMA(())],
        compiler_params=_cparams(("arbitrary",)),
        name="dispatch",
    )(fill_tiles, pos0.reshape(n // tb, 1, tb), pos1.reshape(n // tb, 1, tb), h)


def _combine_body(p0_ref, p1_ref, o_hbm, y_ref, gw_ref, mod_ref, outp_ref, outs_ref, buf0, buf1, sem, *, tc, npt):
    def issue(t, c):
        pltpu.make_async_copy(o_hbm.at[pl.ds(p0_ref[0, 0, t], 1)], buf0.at[pl.ds(t, 1)], sem).start()
        pltpu.make_async_copy(o_hbm.at[pl.ds(p1_ref[0, 0, t], 1)], buf1.at[pl.ds(t, 1)], sem).start()
        return c

    lax.fori_loop(0, tc, issue, 0)
    pltpu.make_async_copy(o_hbm.at[pl.ds(0, tc)], buf0, sem).wait()
    pltpu.make_async_copy(o_hbm.at[pl.ds(0, tc)], buf1, sem).wait()
    gw = gw_ref[...]
    mix = gw[:, 0:1] * buf0[...] + gw[:, 1:2] * buf1[...]
    res = y_ref[...] + mod_ref[0, 5:6, :] * mix
    i = pl.program_id(0)

    @pl.when(i < npt)
    def _():
        outp_ref[...] = res

    @pl.when(i >= npt)
    def _():
        outs_ref[...] = res


def _combine(o_sorted, pos0, pos1, y, gw, mod_l, n_prompt, ts):
    n, d = y.shape
    tc = min(256, ts)
    npt = n_prompt // tc
    pos0, pos1 = pos0.reshape(n // tc, 1, tc), pos1.reshape(n // tc, 1, tc)
    return pl.pallas_call(
        functools.partial(_combine_body, tc=tc, npt=npt),
        grid=(n // tc,),
        in_specs=[
            pl.BlockSpec((1, 1, tc), lambda i: (i, 0, 0), memory_space=pltpu.SMEM),
            pl.BlockSpec((1, 1, tc), lambda i: (i, 0, 0), memory_space=pltpu.SMEM),
            pl.BlockSpec(memory_space=pl.ANY),
            pl.BlockSpec((tc, d), lambda i: (i, 0)),
            pl.BlockSpec((tc, LANES), lambda i: (i, 0)),
            pl.BlockSpec((1, 6, d), lambda i: (_cond_of_rows(i * tc, n_prompt, ts), 0, 0)),
        ],
        out_specs=[pl.BlockSpec((tc, d), lambda i: (jnp.minimum(i, npt - 1), 0)),
                   pl.BlockSpec((tc, d), lambda i: (jnp.maximum(i - npt, 0), 0))],
        out_shape=[jax.ShapeDtypeStruct((n_prompt, d), F32), jax.ShapeDtypeStruct((n - n_prompt, d), F32)],
        scratch_shapes=[pltpu.VMEM((tc, d), F32), pltpu.VMEM((tc, d), F32), pltpu.SemaphoreType.DMA(())],
        compiler_params=_cparams(("arbitrary",)),
        name="combine",
    )(pos0, pos1, o_sorted, y, gw, mod_l)


def _routing_tables(top_i, n_experts, tm, n_tiles_max):
    n = top_i.shape[0]
    flat = jnp.concatenate([top_i[:, 0], top_i[:, 1]])
    onehot = (flat[:, None] == jnp.arange(n_experts, dtype=jnp.int32)[None, :]).astype(jnp.int32)
    csum = jnp.cumsum(onehot, axis=0)
    rank = jnp.sum((csum - onehot) * onehot, axis=1)
    counts = csum[-1]
    tiles = (counts + tm - 1) // tm
    tile_end = jnp.cumsum(tiles)
    start = (tile_end - tiles) * tm
    pos = jnp.sum(onehot * start[None, :], axis=1) + rank
    n_tiles = tile_end[-1]
    j = jnp.arange(n_tiles_max, dtype=jnp.int32)
    tile_expert = jnp.sum((j[:, None] >= tile_end[None, :]).astype(jnp.int32), axis=1)
    tile_expert = jnp.minimum(tile_expert, n_experts - 1).astype(jnp.int32)
    group_last = jnp.where(tiles > 0, tile_end - 1, -1)
    tail = n_tiles + jnp.arange(n_experts, dtype=jnp.int32)
    tail = jnp.where(tail < n_tiles_max, tail, -1)
    fill_tiles = jnp.concatenate([group_last, tail]).astype(jnp.int32)
    return (pos[:n].astype(jnp.int32), pos[n:].astype(jnp.int32), tile_expert,
            n_tiles.reshape(1).astype(jnp.int32), fill_tiles)


def kernel(x_prompt, x_sample, cache_k, cache_v, state_hgrn, c, c_ctx, norm1_g, norm2_g, w_ada, b_ada, w_in,
           hg_lb, hg_onorm_g, na_qn_g, na_kn_g, na_rpb, w_hb, w_nb, w_out, ffn_wg, ffn_wu, ffn_wd,
           moe_router, moe_wg, moe_wu, moe_wd):
    bp, tp, d = x_prompt.shape
    bs, ts, _ = x_sample.shape
    depth = w_in.shape[0]
    hg_heads, dh = state_hgrn.shape[3], state_hgrn.shape[4]
    na_heads = cache_k.shape[2]
    hw, nw = hg_heads * dh, na_heads * dh
    n_experts = moe_wg.shape[1]
    n_prompt, n_sample = bp * tp, bs * ts
    n = n_prompt + n_sample
    assert n_prompt % ts == 0 and ts % tp == 0 and bs + 1 <= N_COND
    col_nq = 5 * hw
    col_ga = 5 * hw + 3 * nw

    p = jax.nn.softmax(hg_lb.astype(F32), axis=1)
    cs = jnp.cumsum(p, axis=1)
    lbs = cs - cs[:, :1]
    bias = _na_bias(na_rpb, ts)
    w_in_b, w_hb_b, w_nb_b, w_out_b = (a.astype(BF16) for a in (w_in, w_hb, w_nb, w_out))
    ffn_b = tuple(a.astype(BF16) for a in (ffn_wg, ffn_wu, ffn_wd))
    n_ffe = moe_wg.shape[3]

    cond = jnp.concatenate([c_ctx[None, :], c, jnp.zeros((N_COND - 1 - bs, d), F32)], axis=0)
    mod = _modulation(cond, w_ada, b_ada).reshape(depth, N_COND, 6, d)

    y = (x_prompt.reshape(n_prompt, d), x_sample.reshape(n_sample, d))
    tm_moe = min(512, n)
    n_tiles_max = (2 * n) // tm_moe + n_experts
    ks_out, vs_out, ss_out = [], [], []
    moe_b = None
    for l in range(depth):
        i = l // 2
        cast_next = l % 2 == 0 and l + 1 < depth
        h = _prenorm(y, norm1_g[l], mod[l], n, n_prompt, ts)
        z, cast_g = _matmul(h, w_in_b[l], (moe_wg[i].reshape(n_experts * d, n_ffe),) if cast_next else ())
        ya_p, s_ctx = _hgrn(z, lbs[:, l], hg_onorm_g[l], row_blk0=0, nb=bp, seq=tp, heads=hg_heads, dh=dh,
                            layer=l, want_state=True)
        ya_s = _hgrn(z, lbs[:, l], hg_onorm_g[l], row_blk0=n_prompt // ts, nb=bs, seq=ts, heads=hg_heads,
                     dh=dh, layer=l, s0=state_hgrn)
        yb_p, k_new, v_new = _ctx_attention(z, na_qn_g[l], na_kn_g[l], nb=bp, seq=tp, heads=na_heads, dh=dh,
                                            col0=col_nq)
        yb_s = _na_attention(z, cache_k, cache_v, bias[l], na_qn_g[l], na_kn_g[l], row_blk0=n_prompt // ts,
                             nb=bs, seq=ts, heads=na_heads, dh=dh, col0=col_nq, layer=l)
        ks_out.append(k_new)
        vs_out.append(v_new)
        ss_out.append(s_ctx)
        m = _merge((ya_p, ya_s), (yb_p, yb_s), w_hb_b[l], w_nb_b[l], z, col_ga, n_prompt, ts)
        if l % 2 == 0:
            y1, h2 = _outproj(m, w_out_b[l], y, mod[l], norm2_g[l], n_prompt, ts)
            tm = min(512, ts)
            riders = (moe_wu[i].reshape(n_experts * d, n_ffe), moe_wd[i].reshape(n_experts * n_ffe, d))
            y, cast_ud = _ffn2(h2, ffn_b[0][i][None], ffn_b[1][i][None], ffn_b[2][i][None],
                               jnp.zeros((n // tm,), jnp.int32), jnp.full((1,), n // tm, jnp.int32),
                               tm=tm, tf=512, tn=512, riders=riders if cast_next else (),
                               residual=(y1, mod[l][:, 5].reshape(N_COND, 1, d), n_prompt, ts))
            if cast_next:
                moe_b = (cast_g[0].reshape(n_experts, d, n_ffe), cast_ud[0].reshape(n_experts, d, n_ffe),
                         cast_ud[1].reshape(n_experts, n_ffe, d))
        else:
            y1, h2, gw, gi = _outproj(m, w_out_b[l], y, mod[l], norm2_g[l], n_prompt, ts, router=moe_router[i])
            pos0, pos1, tile_expert, n_tiles, fill_tiles = _routing_tables(gi[:, :2], n_experts, tm_moe,
                                                                          n_tiles_max)
            xs = _dispatch(h2, pos0, pos1, fill_tiles, n_tiles_max * tm_moe, min(1024, ts), tm_moe)
            o_sorted, _ = _ffn2(xs, *moe_b, tile_expert, n_tiles, tm=tm_moe, tf=1024, tn=512)
            y = tuple(_combine(o_sorted, pos0, pos1, y1, gw, mod[l], n_prompt, ts))

    if not isinstance(y, tuple):
        y = (y[:n_prompt], y[n_prompt:])
    new_cache_k = jnp.concatenate(ks_out, axis=1)
    new_cache_v = jnp.concatenate(vs_out, axis=1)
    new_state_hgrn = jnp.concatenate(ss_out, axis=1)
    return (y[0].reshape(bp, tp, d), y[1].reshape(bs, ts, d), new_cache_k, new_cache_v, new_state_hgrn)
```

```python
import functools

import numpy as np
import jax
import jax.numpy as jnp
from jax import lax
from jax.experimental import pallas as pl
from jax.experimental.pallas import tpu as pltpu

F32 = jnp.float32
BF16 = jnp.bfloat16

NORM_EPS = 1e-6
NEG_INF = -1e30
GRID_W = 64
NA_QB = 16
LANES = 128
N_COND = 16
HG_BLOCK = 32
HG_HALF = HG_BLOCK // 2
VMEM_LIMIT = 56 * 1024 * 1024


def _cparams(semantics, vmem=VMEM_LIMIT):
    return pltpu.CompilerParams(dimension_semantics=semantics, vmem_limit_bytes=vmem)


def _silu(x):
    return x * jax.nn.sigmoid(x)


def _bdot(a, b):
    return jnp.dot(a.astype(BF16), b.astype(BF16), preferred_element_type=F32)


def _bdot_nt(a, b):
    return lax.dot_general(a.astype(BF16), b.astype(BF16), (((1,), (1,)), ((), ())),
                           preferred_element_type=F32)


def _bdot_tn(a, b):
    return lax.dot_general(a.astype(BF16), b.astype(BF16), (((0,), (0,)), ((), ())),
                           preferred_element_type=F32)


def _pick(n, pref):
    if n <= pref:
        return n
    t = pref - pref % LANES
    while n % t:
        t -= LANES
    return t


def _cond_of_rows(row0, n_prompt, ts):
    return jnp.where(row0 < n_prompt, 0, 1 + (row0 - n_prompt) // ts)


def _mod_body(c_ref, w_ref, b_ref, o_ref):
    o_ref[0] = _bdot(_silu(c_ref[...]), w_ref[0]) + b_ref[0]


def _modulation(cond, w_ada, b_ada):
    depth, d, n6 = w_ada.shape
    tn = _pick(n6, 1024)
    return pl.pallas_call(
        _mod_body,
        grid=(depth, n6 // tn),
        in_specs=[
            pl.BlockSpec((N_COND, d), lambda l, j: (0, 0)),
            pl.BlockSpec((1, d, tn), lambda l, j: (l, 0, j)),
            pl.BlockSpec((1, 1, tn), lambda l, j: (l, 0, j)),
        ],
        out_specs=pl.BlockSpec((1, N_COND, tn), lambda l, j: (l, 0, j)),
        out_shape=jax.ShapeDtypeStruct((depth, N_COND, n6), F32),
        compiler_params=_cparams(("parallel", "parallel")),
        name="modulation",
    )(cond, w_ada, b_ada.reshape(depth, 1, n6))


def _rms_modulate(x, g, shift, scale):
    ms = jnp.mean(x * x, axis=-1, keepdims=True)
    return (x * lax.rsqrt(ms + NORM_EPS) * g) * (1.0 + scale) + shift


def _stream_specs(y, tm, n_prompt):
    if not isinstance(y, tuple):
        return [pl.BlockSpec((tm, y.shape[1]), lambda j: (j, 0))], [y]
    d = y[0].shape[1]
    npt = n_prompt // tm
    return ([pl.BlockSpec((tm, d), lambda j: (jnp.minimum(j, npt - 1), 0)),
             pl.BlockSpec((tm, d), lambda j: (jnp.maximum(j - npt, 0), 0))], list(y))


def _stream_tile(refs, npt):
    if len(refs) == 1:
        return refs[0][...]
    return jnp.where(pl.program_id(0) < npt, refs[0][...], refs[1][...])


def _prenorm_body(*refs, npt):
    g_ref, m_ref, o_ref = refs[-3:]
    y = _stream_tile(refs[:-3], npt)
    o_ref[...] = _rms_modulate(y, g_ref[...], m_ref[0, 0:1, :], m_ref[0, 1:2, :]).astype(o_ref.dtype)


def _prenorm(y, g, mod_l, n, n_prompt, ts):
    d = g.shape[0]
    tm = min(512, ts)
    y_specs, y_args = _stream_specs(y, tm, n_prompt)
    return pl.pallas_call(
        functools.partial(_prenorm_body, npt=n_prompt // tm),
        grid=(n // tm,),
        in_specs=y_specs + [
            pl.BlockSpec((1, d), lambda j: (0, 0)),
            pl.BlockSpec((1, 6, d), lambda j: (_cond_of_rows(j * tm, n_prompt, ts), 0, 0)),
        ],
        out_specs=pl.BlockSpec((tm, d), lambda j: (j, 0)),
        out_shape=jax.ShapeDtypeStruct((n, d), BF16),
        compiler_params=_cparams(("parallel",)),
        name="prenorm",
    )(*y_args, g.reshape(1, d), mod_l)


def _rider_specs(riders, n_steps, step_of):
    in_specs, out_specs, out_shape = [], [], []
    for r, n_split in riders:
        rows, cols = r.shape
        assert rows % (n_steps * 16) == 0 and cols % (n_split * LANES) == 0, (r.shape, n_steps, n_split)
        in_specs.append(pl.BlockSpec((rows // n_steps, cols), lambda *g: (step_of(*g), 0)))
        out_specs.append(pl.BlockSpec((n_split, rows // n_steps, cols // n_split), lambda *g: (0, step_of(*g), 0)))
        out_shape.append(jax.ShapeDtypeStruct((n_split, rows, cols // n_split), BF16))
    return in_specs, out_specs, out_shape


def _rider_copy(srcs, dsts):
    for src, dst in zip(srcs, dsts):
        w = dst.shape[2]
        for c in range(dst.shape[0]):
            dst[c] = src[:, c * w:(c + 1) * w].astype(dst.dtype)


def _mm_body(x_ref, w_ref, *refs):
    n_r = (len(refs) - 1) // 2
    o_ref = refs[n_r]
    o_ref[...] = jnp.dot(x_ref[...], w_ref[...], preferred_element_type=F32).astype(o_ref.dtype)
    _rider_copy(refs[:n_r], refs[n_r + 1:])


def _matmul(x, w, riders=()):
    m, k = x.shape
    n = w.shape[1]
    tm, tn = _pick(m, 1024), _pick(n, 1536)
    gm, gn = m // tm, n // tn
    r_in, r_out, r_shape = _rider_specs(riders, gm * gn, lambda i, j: i * gn + j)
    res = pl.pallas_call(
        _mm_body,
        grid=(gm, gn),
        in_specs=[pl.BlockSpec((tm, k), lambda i, j: (i, 0)), pl.BlockSpec((k, tn), lambda i, j: (0, j))] + r_in,
        out_specs=[pl.BlockSpec((tm, tn), lambda i, j: (i, j))] + r_out,
        out_shape=[jax.ShapeDtypeStruct((m, n), F32)] + r_shape,
        compiler_params=_cparams(("parallel", "parallel")),
        name="in_proj",
    )(x, w, *[r for r, _ in riders])
    return res[0], res[1:]


def _hgrn_gates(z, lb):
    t = jnp.exp(-jnp.abs(z))
    pos = z >= 0.0
    inv = 1.0 / (1.0 + t)
    f = jnp.where(pos, 1.0 + lb * t, lb + t) * inv
    k = (1.0 - lb) * jnp.where(pos, t, 1.0) * inv
    return jnp.log(f), k


def _split2(x):
    hi = x.astype(BF16)
    lo = (x - hi.astype(F32)).astype(BF16)
    return hi, lo


def _hgrn_body(*refs, seq, hp, dh, has_s0, has_sfin, unroll):
    zq, zff, zfb, zi, zg, lb_ref, og_ref = refs[:7]
    pos = 7
    s0_ref = None
    if has_s0:
        s0_ref = refs[pos]
        pos += 1
    ya_ref = refs[pos]
    pos += 1
    sfin_ref = None
    if has_sfin:
        sfin_ref = refs[pos]
        pos += 1
    o_s, st_s = refs[pos:pos + 2]
    w = hp * dh
    nblk = seq // HG_BLOCK
    zf = (zff, zfb)

    for d in range(2):
        for h in range(hp):
            if has_s0:
                st_s[d * hp + h] = s0_ref[0, 0, d, h].T
            else:
                st_s[d * hp + h] = jnp.zeros((dh, dh), F32)

    row = lax.broadcasted_iota(jnp.int32, (HG_BLOCK, HG_BLOCK), 0)
    col = lax.broadcasted_iota(jnp.int32, (HG_BLOCK, HG_BLOCK), 1)
    same_half = (row // HG_HALF) == (col // HG_HALF)
    rowv = lax.broadcasted_iota(jnp.int32, (HG_BLOCK, 1), 0)
    attend = (col <= row, col >= row)
    seg = tuple(jnp.where(a & same_half, 1.0, 0.0).astype(BF16) for a in attend)
    first_half = (rowv < HG_HALF, rowv >= HG_HALF)
    first_end = (HG_HALF - 1, HG_HALF)
    second_end = (HG_BLOCK - 1, 0)

    units = [(d, h) for d in range(2) for h in range(hp)]

    def block(i, carry):
        rows = (pl.ds(pl.multiple_of(i * HG_BLOCK, HG_BLOCK), HG_BLOCK),
                pl.ds(pl.multiple_of(seq - HG_BLOCK * (i + 1), HG_BLOCK), HG_BLOCK))
        cum2, kbs = [], []
        for d in range(2):
            lf, kb = _hgrn_gates(zf[d][rows[d], :], lb_ref[d:d + 1, :])
            kbs.append(kb)
            hi, lo = _split2(lf)
            cum2.append(jnp.dot(seg[d], jnp.concatenate([hi, lo], axis=1), preferred_element_type=F32))
        vb = [zi[rows[d], :].astype(BF16) for d in range(2)]
        qe, ke, qs, kend, dec = [], [], [], [], []
        for d in range(2):
            qb = _silu(zq[rows[d], :])
            kb = kbs[d]
            cum = cum2[d][:, :w] + cum2[d][:, w:]
            l_first = cum[first_end[d]:first_end[d] + 1, :]
            l_second = cum[second_end[d]:second_end[d] + 1, :]
            l_blk = l_first + l_second
            cum_blk = cum + jnp.where(first_half[d], 0.0, l_first)
            e = cum_blk - l_first
            qe.append((qb * jnp.exp(e)).astype(BF16))
            ke.append((kb * jnp.exp(-e)).astype(BF16))
            qs.append((qb * jnp.exp(cum_blk)).astype(BF16))
            kend.append((kb * jnp.exp(l_blk - cum_blk)).astype(BF16))
            dec.append(jnp.exp(l_blk))
        sls = [slice(h * dh, (h + 1) * dh) for h in range(hp)]
        a = [_bdot_nt(qe[d][:, sls[h]], ke[d][:, sls[h]]) for d, h in units]
        st = [st_s[d * hp + h] for d, h in units]
        inter = [_bdot_nt(qs[d][:, sls[h]], st[u]) for u, (d, h) in enumerate(units)]
        ds = [_bdot_tn(vb[d][:, sls[h]], kend[d][:, sls[h]]) for d, h in units]
        intra = [_bdot(jnp.where(attend[d], a[u], 0.0), vb[d][:, sls[h]]) for u, (d, h) in enumerate(units)]
        for u, (d, h) in enumerate(units):
            st_s[d * hp + h] = st[u] * dec[d][:, sls[h]] + ds[u]
            o_s[d, rows[d], sls[h]] = intra[u] + inter[u]
        return carry

    lax.fori_loop(0, nblk, block, 0, unroll=unroll)

    def finish(c, carry):
        rows = pl.ds(pl.multiple_of(c * HG_BLOCK, HG_BLOCK), HG_BLOCK)
        o = o_s[0, rows, :] + o_s[1, rows, :]
        gate = _silu(zg[rows, :])
        for h in range(hp):
            sl = slice(h * dh, (h + 1) * dh)
            oh = o[:, sl]
            ms = jnp.mean(oh * oh, axis=-1, keepdims=True)
            ya_ref[rows, sl] = (oh * lax.rsqrt(ms + NORM_EPS) * og_ref[...] * gate[:, sl]).astype(ya_ref.dtype)
        return carry

    lax.fori_loop(0, nblk, finish, 0)
    if has_sfin:
        for d in range(2):
            for h in range(hp):
                sfin_ref[0, 0, d, h] = st_s[d * hp + h].T


def _hgrn(z, lb_l, og_l, *, row_blk0, nb, seq, heads, dh, layer, s0=None, want_state=False, hp=None, unroll=2):
    if hp is None:
        hp = next(c for c in (4, 2, 1) if heads % c == 0)
    w = hp * dh
    hw = heads * dh
    ng = heads // hp
    cb = hw // w

    def zspec(k):
        return pl.BlockSpec((seq, w), lambda b, g, k=k: (row_blk0 + b, k * cb + g))

    in_specs = [zspec(0), zspec(1), zspec(2), zspec(3), zspec(4),
                pl.BlockSpec((2, w), lambda b, g: (0, g)),
                pl.BlockSpec((1, dh), lambda b, g: (0, 0))]
    args = [z, z, z, z, z, lb_l, og_l.reshape(1, dh)]
    if s0 is not None:
        in_specs.append(pl.BlockSpec((1, 1, 2, hp, dh, dh), lambda b, g: (b, layer, 0, g, 0, 0)))
        args.append(s0)
    out_specs = [pl.BlockSpec((seq, w), lambda b, g: (b, g))]
    out_shape = [jax.ShapeDtypeStruct((nb * seq, hw), BF16)]
    if want_state:
        out_specs.append(pl.BlockSpec((1, 1, 2, hp, dh, dh), lambda b, g: (b, 0, 0, g, 0, 0)))
        out_shape.append(jax.ShapeDtypeStruct((nb, 1, 2, heads, dh, dh), F32))
    res = pl.pallas_call(
        functools.partial(_hgrn_body, seq=seq, hp=hp, dh=dh, has_s0=s0 is not None, has_sfin=want_state,
                          unroll=unroll),
        grid=(nb, ng),
        in_specs=in_specs,
        out_specs=out_specs,
        out_shape=out_shape,
        scratch_shapes=[
            pltpu.VMEM((2, seq, w), F32),
            pltpu.VMEM((2 * hp, dh, dh), F32),
        ],
        compiler_params=_cparams(("parallel", "parallel")),
        name="hgrn",
    )(*args)
    return res if want_state else res[0]


def _head_rms(x, g):
    ms = jnp.mean(x * x, axis=-1, keepdims=True)
    return x * lax.rsqrt(ms + NORM_EPS) * g


def _ctx_attn_body(zq, zk, zv, qg_ref, kg_ref, yb_ref, k_ref, v_ref, *, hp, dh):
    scale = dh ** -0.5
    for h in range(hp):
        sl = slice(h * dh, (h + 1) * dh)
        q = _head_rms(zq[:, sl], qg_ref[...])
        k = _head_rms(zk[:, sl], kg_ref[...])
        v = zv[:, sl]
        k_ref[0, 0, h] = k
        v_ref[0, 0, h] = v
        s = _bdot_nt(q, k) * scale
        p = jnp.exp(s - jnp.max(s, axis=-1, keepdims=True))
        o = _bdot(p, v) / jnp.sum(p, axis=-1, keepdims=True)
        yb_ref[:, sl] = o.astype(yb_ref.dtype)


def _ctx_attention(z, qg, kg, *, nb, seq, heads, dh, col0):
    hp = 2 if heads % 2 == 0 else 1
    w = hp * dh
    nw = heads * dh
    ng = heads // hp
    c0 = col0 // w
    cb = nw // w

    def zspec(k):
        return pl.BlockSpec((seq, w), lambda b, g, k=k: (b, c0 + k * cb + g))

    kv_spec = pl.BlockSpec((1, 1, hp, seq, dh), lambda b, g: (b, 0, g, 0, 0))
    kv_shape = jax.ShapeDtypeStruct((nb, 1, heads, seq, dh), F32)
    return pl.pallas_call(
        functools.partial(_ctx_attn_body, hp=hp, dh=dh),
        grid=(nb, ng),
        in_specs=[zspec(0), zspec(1), zspec(2),
                  pl.BlockSpec((1, dh), lambda b, g: (0, 0)), pl.BlockSpec((1, dh), lambda b, g: (0, 0))],
        out_specs=[pl.BlockSpec((seq, w), lambda b, g: (b, g)), kv_spec, kv_spec],
        out_shape=[jax.ShapeDtypeStruct((nb * seq, nw), BF16), kv_shape, kv_shape],
        compiler_params=_cparams(("parallel", "parallel")),
        name="ctx_attention",
    )(z, z, z, qg.reshape(1, dh), kg.reshape(1, dh))


def _na_bias(rpb, seq):
    win_r, win_c = (rpb.shape[2] + 1) // 2, (rpb.shape[3] + 1) // 2
    rows = seq // GRID_W
    wr = min(win_r, rows)
    r, c = np.arange(rows), np.arange(GRID_W)
    r0 = np.clip(r - wr // 2, 0, rows - wr)
    c0 = np.clip(c - win_c // 2, 0, GRID_W - win_c)
    row_ok = (r[None, :] >= r0[:, None]) & (r[None, :] < r0[:, None] + wr)
    col_ok = (c[None, :] >= c0[:, None]) & (c[None, :] < c0[:, None] + win_c)
    valid = (row_ok[:, None, :, None] & col_ok[None, :, None, :]).reshape(seq, seq)
    sel_r = (r[None, :, None] - r[:, None, None] + win_r - 1 == np.arange(2 * win_r - 1)).astype(np.float32)
    sel_c = (c[None, :, None] - c[:, None, None] + win_c - 1 == np.arange(2 * win_c - 1)).astype(np.float32)
    table = jnp.einsum("xka,lhab,cqb->lhxckq", sel_r, rpb.astype(F32), sel_c, precision=lax.Precision.HIGHEST)
    table = table.reshape(rpb.shape[0], rpb.shape[1], seq, seq)
    return jnp.where(valid[None, None], table, NEG_INF)


def _na_attn_body(zq, zk, zv, kc_ref, vc_ref, bias_ref, qg_ref, kg_ref, yb_ref, *, seq, dh, tq):
    scale = dh ** -0.5
    q = _head_rms(zq[...], qg_ref[...]).astype(BF16)
    k = _head_rms(zk[...], kg_ref[...]).astype(BF16)
    v = zv[...].astype(BF16)
    kc = kc_ref[0, 0, 0].astype(BF16)
    vc = vc_ref[0, 0, 0].astype(BF16)
    for i in range(seq // tq):
        rows = slice(i * tq, (i + 1) * tq)
        qt = q[rows]
        s_win = _bdot_nt(qt, k) * scale + bias_ref[0, rows, :]
        s_ctx = _bdot_nt(qt, kc) * scale
        m = jnp.maximum(jnp.max(s_win, axis=-1, keepdims=True), jnp.max(s_ctx, axis=-1, keepdims=True))
        p_win = jnp.exp(s_win - m)
        p_ctx = jnp.exp(s_ctx - m)
        den = jnp.sum(p_win, axis=-1, keepdims=True) + jnp.sum(p_ctx, axis=-1, keepdims=True)
        o = (_bdot(p_win, v) + _bdot(p_ctx, vc)) / den
        yb_ref[rows, :] = o.astype(yb_ref.dtype)


def _na_attention(z, cache_k, cache_v, bias, qg, kg, *, row_blk0, nb, seq, heads, dh, col0, layer):
    nw = heads * dh
    c0 = col0 // dh
    past = cache_k.shape[3]

    def zspec(k):
        return pl.BlockSpec((seq, dh), lambda h, b, k=k: (row_blk0 + b, c0 + k * heads + h))

    cache_spec = pl.BlockSpec((1, 1, 1, past, dh), lambda h, b: (b, layer, h, 0, 0))
    return pl.pallas_call(
        functools.partial(_na_attn_body, seq=seq, dh=dh, tq=min(256, seq)),
        grid=(heads, nb),
        in_specs=[zspec(0), zspec(1), zspec(2), cache_spec, cache_spec,
                  pl.BlockSpec((1, seq, seq), lambda h, b: (h, 0, 0)),
                  pl.BlockSpec((1, dh), lambda h, b: (0, 0)), pl.BlockSpec((1, dh), lambda h, b: (0, 0))],
        out_specs=pl.BlockSpec((seq, dh), lambda h, b: (b, h)),
        out_shape=jax.ShapeDtypeStruct((nb * seq, nw), BF16),
        compiler_params=_cparams(("parallel", "parallel")),
        name="na_attention",
    )(z, z, z, cache_k, cache_v, bias, qg.reshape(1, dh), kg.reshape(1, dh))


def _merge_body(yap_ref, yas_ref, ybp_ref, ybs_ref, wh_ref, wn_ref, ga_ref, gb_ref, o_ref, *, npt):
    is_prompt = pl.program_id(0) < npt
    ya = jnp.where(is_prompt, yap_ref[...], yas_ref[...])
    yb = jnp.where(is_prompt, ybp_ref[...], ybs_ref[...])
    a = jnp.dot(ya, wh_ref[...], preferred_element_type=F32)
    b = jnp.dot(yb, wn_ref[...], preferred_element_type=F32)
    o_ref[...] = (jax.nn.sigmoid(ga_ref[...]) * a + jax.nn.sigmoid(gb_ref[...]) * b).astype(o_ref.dtype)


def _merge(ya, yb, w_hb, w_nb, z, col_ga, n_prompt, ts):
    n = z.shape[0]
    hw, nw = ya[0].shape[1], yb[0].shape[1]
    d = w_hb.shape[1]
    tm, tn = min(512, ts), _pick(d, 1024)
    npt = n_prompt // tm
    ca, cbk = col_ga // tn, (col_ga + d) // tn

    def split(width):
        return [pl.BlockSpec((tm, width), lambda i, j: (jnp.minimum(i, npt - 1), 0)),
                pl.BlockSpec((tm, width), lambda i, j: (jnp.maximum(i - npt, 0), 0))]

    return pl.pallas_call(
        functools.partial(_merge_body, npt=npt),
        grid=(n // tm, d // tn),
        in_specs=split(hw) + split(nw) + [
            pl.BlockSpec((hw, tn), lambda i, j: (0, j)),
            pl.BlockSpec((nw, tn), lambda i, j: (0, j)),
            pl.BlockSpec((tm, tn), lambda i, j: (i, ca + j)),
            pl.BlockSpec((tm, tn), lambda i, j: (i, cbk + j)),
        ],
        out_specs=pl.BlockSpec((tm, tn), lambda i, j: (i, j)),
        out_shape=jax.ShapeDtypeStruct((n, d), BF16),
        compiler_params=_cparams(("parallel", "parallel")),
        name="merge",
    )(*ya, *yb, w_hb, w_nb, z, z)


def _outproj_body(*refs, n_experts, n_stream, npt):
    y = _stream_tile(refs[:n_stream], npt)
    m_ref, w_ref, mod_ref, g_ref = refs[n_stream:n_stream + 4]
    if n_experts:
        r_ref, y1_ref, h2_ref, gw_ref, gi_ref = refs[n_stream + 4:]
    else:
        y1_ref, h2_ref = refs[n_stream + 4:]
    y1 = y + mod_ref[0, 2:3, :] * jnp.dot(m_ref[...], w_ref[...], preferred_element_type=F32)
    y1_ref[...] = y1
    h2 = _rms_modulate(y1, g_ref[...], mod_ref[0, 3:4, :], mod_ref[0, 4:5, :])
    h2_ref[...] = h2.astype(h2_ref.dtype)
    if n_experts:
        r = r_ref[...]
        h_hi = h2.astype(BF16)
        h_lo = (h2 - h_hi.astype(F32)).astype(BF16)
        r_hi = r.astype(BF16)
        r_lo = (r - r_hi.astype(F32)).astype(BF16)
        logits = (jnp.dot(h_hi, r_hi, preferred_element_type=F32)
                  + jnp.dot(h_hi, r_lo, preferred_element_type=F32)
                  + jnp.dot(h_lo, r_hi, preferred_element_type=F32))
        lane = lax.broadcasted_iota(jnp.int32, logits.shape, 1).astype(F32)
        big = float(LANES)
        lg = jnp.where(lane < n_experts, logits, -jnp.inf)
        m1 = jnp.max(lg, axis=-1, keepdims=True)
        i1 = jnp.min(jnp.where(lg == m1, lane, big), axis=-1, keepdims=True)
        lg2 = jnp.where(lane == i1, -jnp.inf, lg)
        m2 = jnp.max(lg2, axis=-1, keepdims=True)
        i2 = jnp.min(jnp.where(lg2 == m2, lane, big), axis=-1, keepdims=True)
        t = jnp.exp(m2 - m1)
        w1 = 1.0 / (1.0 + t)
        w2 = t / (1.0 + t)
        gw_ref[...] = jnp.where(lane == 0.0, w1, jnp.where(lane == 1.0, w2, 0.0))
        gi_ref[...] = jnp.where(lane == 0.0, i1, jnp.where(lane == 1.0, i2, 0.0)).astype(jnp.int32)


def _outproj(m, w_out, y, mod_l, g2, n_prompt, ts, router=None):
    n, d = m.shape
    tm = min(256, ts)
    n_experts = 0 if router is None else router.shape[1]
    y_specs, y_args = _stream_specs(y, tm, n_prompt)
    in_specs = y_specs + [
        pl.BlockSpec((tm, d), lambda j: (j, 0)),
        pl.BlockSpec((d, d), lambda j: (0, 0)),
        pl.BlockSpec((1, 6, d), lambda j: (_cond_of_rows(j * tm, n_prompt, ts), 0, 0)),
        pl.BlockSpec((1, d), lambda j: (0, 0)),
    ]
    args = y_args + [m, w_out, mod_l, g2.reshape(1, d)]
    row_spec = pl.BlockSpec((tm, d), lambda j: (j, 0))
    out_specs = [row_spec, row_spec]
    out_shape = [jax.ShapeDtypeStruct((n, d), F32), jax.ShapeDtypeStruct((n, d), F32 if n_experts else BF16)]
    if n_experts:
        in_specs.append(pl.BlockSpec((d, LANES), lambda j: (0, 0)))
        args.append(jnp.pad(router, ((0, 0), (0, LANES - n_experts))))
        lane_spec = pl.BlockSpec((tm, LANES), lambda j: (j, 0))
        out_specs += [lane_spec, lane_spec]
        out_shape += [jax.ShapeDtypeStruct((n, LANES), F32), jax.ShapeDtypeStruct((n, LANES), jnp.int32)]
    return pl.pallas_call(
        functools.partial(_outproj_body, n_experts=n_experts, n_stream=len(y_args), npt=n_prompt // tm),
        grid=(n // tm,),
        in_specs=in_specs,
        out_specs=out_specs,
        out_shape=out_shape,
        compiler_params=_cparams(("parallel",)),
        name="outproj",
    )(*args)


def _ffn2_body(te_ref, nt_ref, *refs, residual, nf, nn, tf, n_riders):
    del te_ref
    x_ref, wg_ref, wu_ref, wd_ref = refs[:4]
    pos = 4
    if residual:
        y_ref, g2_ref = refs[4:6]
        pos = 6
    r_src = refs[pos:pos + n_riders]
    o_ref = refs[pos + n_riders]
    r_dst = refs[pos + n_riders + 1:pos + 2 * n_riders + 1]
    act_s = refs[-1]
    j, s = pl.program_id(0), pl.program_id(1)
    live = j < nt_ref[0]
    down = (s >= nf) & (s < nf + nn)
    _rider_copy(r_src, r_dst)

    @pl.when(live & (s < nf))
    def _():
        x = x_ref[...].astype(BF16)
        g = jnp.dot(x, wg_ref[0, 0], preferred_element_type=F32)
        u = jnp.dot(x, wu_ref[0, 0], preferred_element_type=F32)
        act_s[s] = (_silu(g) * u).astype(BF16)

    @pl.when(live & down)
    def _():
        r = jnp.dot(act_s[0], wd_ref[0, 0, 0:tf, :], preferred_element_type=F32)
        for c in range(1, nf):
            r += jnp.dot(act_s[c], wd_ref[0, 0, c * tf:(c + 1) * tf, :], preferred_element_type=F32)
        if residual:
            r = y_ref[...] + g2_ref[0] * r
        o_ref[...] = r

    @pl.when(jnp.logical_not(live) & down)
    def _():
        o_ref[...] = jnp.zeros_like(o_ref)


def _tile_cols(w, n_split):
    e, rows, cols = w.shape
    return w.reshape(e, rows, n_split, cols // n_split).transpose(2, 0, 1, 3)


def _ffn2(x, wg, wu, wd, tile_expert, n_tiles, *, tm, residual=None, riders=()):
    n, d = x.shape
    nf, _, _, tf = wg.shape
    nn, _, ff, tn = wd.shape
    gm, gs = n // tm, nf + nn
    while any(r.shape[0] % (gm * gs * 16) for r, _ in riders):
        gs += 1

    def row(j, nt):
        return jnp.minimum(j, nt[0] - 1)

    def slab(s):
        return jnp.clip(s - nf, 0, nn - 1)

    def up_idx(j, s, te, nt):
        return (jnp.where(j < nt[0], jnp.minimum(s, nf - 1), nf - 1), te[row(j, nt)], 0, 0)

    def down_idx(j, s, te, nt):
        return (jnp.where(j < nt[0], slab(s), nn - 1), te[row(j, nt)], 0, 0)

    in_specs = [
        pl.BlockSpec((tm, d), lambda j, s, te, nt: (row(j, nt), 0)),
        pl.BlockSpec((1, 1, d, tf), up_idx),
        pl.BlockSpec((1, 1, d, tf), up_idx),
        pl.BlockSpec((1, 1, ff, tn), down_idx),
    ]
    args = [x, wg, wu, wd]
    if residual is not None:
        y, gate2, n_prompt, ts = residual
        in_specs += [
            pl.BlockSpec((tm, tn), lambda j, s, te, nt: (row(j, nt), slab(s))),
            pl.BlockSpec((1, 1, tn), lambda j, s, te, nt: (_cond_of_rows(j * tm, n_prompt, ts), 0, slab(s))),
        ]
        args += [y, gate2]
    r_in, r_out, r_shape = _rider_specs(riders, gm * gs, lambda j, s, te, nt: j * gs + s)
    res = pl.pallas_call(
        functools.partial(_ffn2_body, residual=residual is not None, nf=nf, nn=nn, tf=tf, n_riders=len(riders)),
        grid_spec=pltpu.PrefetchScalarGridSpec(
            num_scalar_prefetch=2,
            grid=(gm, gs),
            in_specs=in_specs + r_in,
            out_specs=[pl.BlockSpec((tm, tn), lambda j, s, te, nt: (j, slab(s)))] + r_out,
            scratch_shapes=[pltpu.VMEM((nf, tm, tf), BF16)],
        ),
        out_shape=[jax.ShapeDtypeStruct((n, d), F32)] + r_shape,
        compiler_params=_cparams(("arbitrary", "arbitrary")),
        name="swiglu",
    )(tile_expert, n_tiles, *args, *[r for r, _ in riders])
    return res[0], res[1:]


def _dispatch_body(fill_ref, p0_ref, p1_ref, src, dst, zero_s, sem, zsem, *, tb, tm):
    @pl.when(pl.program_id(0) == 0)
    def _():
        zero_s[...] = jnp.zeros_like(zero_s)

        def fill(u):
            row0 = pl.multiple_of(jnp.maximum(fill_ref[u], 0) * tm, tm)
            return pltpu.make_async_copy(zero_s, dst.at[pl.ds(row0, tm)], zsem)

        for u in range(fill_ref.shape[0]):
            pl.when(fill_ref[u] >= 0)(fill(u).start)
        for u in range(fill_ref.shape[0]):
            pl.when(fill_ref[u] >= 0)(fill(u).wait)

    def issue(t, c):
        row = src.at[pl.ds(t, 1)]
        pltpu.make_async_copy(row, dst.at[pl.ds(p0_ref[0, 0, t], 1)], sem).start()
        pltpu.make_async_copy(row, dst.at[pl.ds(p1_ref[0, 0, t], 1)], sem).start()
        return c

    lax.fori_loop(0, tb, issue, 0)
    for _ in range(2):
        pltpu.make_async_copy(src, dst.at[pl.ds(0, tb)], sem).wait()


def _dispatch(h, pos0, pos1, fill_tiles, n_sorted, tb, tm):
    n, d = h.shape
    return pl.pallas_call(
        functools.partial(_dispatch_body, tb=tb, tm=tm),
        grid=(n // tb,),
        in_specs=[
            pl.BlockSpec(memory_space=pltpu.SMEM),
            pl.BlockSpec((1, 1, tb), lambda i: (i, 0, 0), memory_space=pltpu.SMEM),
            pl.BlockSpec((1, 1, tb), lambda i: (i, 0, 0), memory_space=pltpu.SMEM),
            pl.BlockSpec((tb, d), lambda i: (i, 0)),
        ],
        out_specs=pl.BlockSpec(memory_space=pl.ANY),
        out_shape=jax.ShapeDtypeStruct((n_sorted, d), h.dtype),
        scratch_shapes=[pltpu.VMEM((tm, d), h.dtype), pltpu.SemaphoreType.DMA(()), pltpu.SemaphoreType.DMA(())],
        compiler_params=_cparams(("arbitrary",)),
        name="dispatch",
    )(fill_tiles, pos0.reshape(n // tb, 1, tb), pos1.reshape(n // tb, 1, tb), h)


def _combine_body(p0_ref, p1_ref, o_hbm, y_ref, gw_ref, mod_ref, outp_ref, outs_ref, buf0, buf1, sem, *, tc, npt):
    def issue(t, c):
        pltpu.make_async_copy(o_hbm.at[pl.ds(p0_ref[0, 0, t], 1)], buf0.at[pl.ds(t, 1)], sem).start()
        pltpu.make_async_copy(o_hbm.at[pl.ds(p1_ref[0, 0, t], 1)], buf1.at[pl.ds(t, 1)], sem).start()
        return c

    lax.fori_loop(0, tc, issue, 0)
    pltpu.make_async_copy(o_hbm.at[pl.ds(0, tc)], buf0, sem).wait()
    pltpu.make_async_copy(o_hbm.at[pl.ds(0, tc)], buf1, sem).wait()
    gw = gw_ref[...]
    mix = gw[:, 0:1] * buf0[...] + gw[:, 1:2] * buf1[...]
    res = y_ref[...] + mod_ref[0, 5:6, :] * mix
    i = pl.program_id(0)

    @pl.when(i < npt)
    def _():
        outp_ref[...] = res

    @pl.when(i >= npt)
    def _():
        outs_ref[...] = res


def _combine(o_sorted, pos0, pos1, y, gw, mod_l, n_prompt, ts):
    n, d = y.shape
    tc = min(256, ts)
    npt = n_prompt // tc
    pos0, pos1 = pos0.reshape(n // tc, 1, tc), pos1.reshape(n // tc, 1, tc)
    return pl.pallas_call(
        functools.partial(_combine_body, tc=tc, npt=npt),
        grid=(n // tc,),
        in_specs=[
            pl.BlockSpec((1, 1, tc), lambda i: (i, 0, 0), memory_space=pltpu.SMEM),
            pl.BlockSpec((1, 1, tc), lambda i: (i, 0, 0), memory_space=pltpu.SMEM),
            pl.BlockSpec(memory_space=pl.ANY),
            pl.BlockSpec((tc, d), lambda i: (i, 0)),
            pl.BlockSpec((tc, LANES), lambda i: (i, 0)),
            pl.BlockSpec((1, 6, d), lambda i: (_cond_of_rows(i * tc, n_prompt, ts), 0, 0)),
        ],
        out_specs=[pl.BlockSpec((tc, d), lambda i: (jnp.minimum(i, npt - 1), 0)),
                   pl.BlockSpec((tc, d), lambda i: (jnp.maximum(i - npt, 0), 0))],
        out_shape=[jax.ShapeDtypeStruct((n_prompt, d), F32), jax.ShapeDtypeStruct((n - n_prompt, d), F32)],
        scratch_shapes=[pltpu.VMEM((tc, d), F32), pltpu.VMEM((tc, d), F32), pltpu.SemaphoreType.DMA(())],
        compiler_params=_cparams(("arbitrary",)),
        name="combine",
    )(pos0, pos1, o_sorted, y, gw, mod_l)


def _routing_tables(top_i, n_experts, tm, n_tiles_max):
    n = top_i.shape[0]
    flat = jnp.concatenate([top_i[:, 0], top_i[:, 1]])
    onehot = (flat[:, None] == jnp.arange(n_experts, dtype=jnp.int32)[None, :]).astype(jnp.int32)
    csum = jnp.cumsum(onehot, axis=0)
    rank = jnp.sum((csum - onehot) * onehot, axis=1)
    counts = csum[-1]
    tiles = (counts + tm - 1) // tm
    tile_end = jnp.cumsum(tiles)
    start = (tile_end - tiles) * tm
    pos = jnp.sum(onehot * start[None, :], axis=1) + rank
    n_tiles = tile_end[-1]
    j = jnp.arange(n_tiles_max, dtype=jnp.int32)
    tile_expert = jnp.sum((j[:, None] >= tile_end[None, :]).astype(jnp.int32), axis=1)
    tile_expert = jnp.minimum(tile_expert, n_experts - 1).astype(jnp.int32)
    group_last = jnp.where(tiles > 0, tile_end - 1, -1)
    tail = n_tiles + jnp.arange(n_experts, dtype=jnp.int32)
    tail = jnp.where(tail < n_tiles_max, tail, -1)
    fill_tiles = jnp.concatenate([group_last, tail]).astype(jnp.int32)
    return (pos[:n].astype(jnp.int32), pos[n:].astype(jnp.int32), tile_expert,
            n_tiles.reshape(1).astype(jnp.int32), fill_tiles)


def kernel(x_prompt, x_sample, cache_k, cache_v, state_hgrn, c, c_ctx, norm1_g, norm2_g, w_ada, b_ada, w_in,
           hg_lb, hg_onorm_g, na_qn_g, na_kn_g, na_rpb, w_hb, w_nb, w_out, ffn_wg, ffn_wu, ffn_wd,
           moe_router, moe_wg, moe_wu, moe_wd):
    bp, tp, d = x_prompt.shape
    bs, ts, _ = x_sample.shape
    depth = w_in.shape[0]
    hg_heads, dh = state_hgrn.shape[3], state_hgrn.shape[4]
    na_heads = cache_k.shape[2]
    hw, nw = hg_heads * dh, na_heads * dh
    n_experts = moe_wg.shape[1]
    n_prompt, n_sample = bp * tp, bs * ts
    n = n_prompt + n_sample
    assert n_prompt % ts == 0 and ts % tp == 0 and bs + 1 <= N_COND
    col_nq = 5 * hw
    col_ga = 5 * hw + 3 * nw

    p = jax.nn.softmax(hg_lb.astype(F32), axis=1)
    cs = jnp.cumsum(p, axis=1)
    lbs = cs - cs[:, :1]
    bias = _na_bias(na_rpb, ts)
    w_in_b, w_hb_b, w_nb_b, w_out_b = (a.astype(BF16) for a in (w_in, w_hb, w_nb, w_out))
    ffn_b = tuple(a.astype(BF16) for a in (ffn_wg, ffn_wu, ffn_wd))
    n_ffe = moe_wg.shape[3]
    nf_moe = n_ffe // _pick(n_ffe, 1024)
    nn_out = d // _pick(d, 512)

    cond = jnp.concatenate([c_ctx[None, :], c, jnp.zeros((N_COND - 1 - bs, d), F32)], axis=0)
    mod = _modulation(cond, w_ada, b_ada).reshape(depth, N_COND, 6, d)

    y = (x_prompt.reshape(n_prompt, d), x_sample.reshape(n_sample, d))
    tm_moe = min(512, n)
    n_tiles_max = (2 * n) // tm_moe + n_experts
    ks_out, vs_out, ss_out = [], [], []
    moe_b = None
    for l in range(depth):
        i = l // 2
        cast_next = l % 2 == 0 and l + 1 < depth
        h = _prenorm(y, norm1_g[l], mod[l], n, n_prompt, ts)
        z, cast_g = _matmul(h, w_in_b[l],
                            ((moe_wg[i].reshape(n_experts * d, n_ffe), nf_moe),) if cast_next else ())
        ya_p, s_ctx = _hgrn(z, lbs[:, l], hg_onorm_g[l], row_blk0=0, nb=bp, seq=tp, heads=hg_heads, dh=dh,
                            layer=l, want_state=True)
        ya_s = _hgrn(z, lbs[:, l], hg_onorm_g[l], row_blk0=n_prompt // ts, nb=bs, seq=ts, heads=hg_heads,
                     dh=dh, layer=l, s0=state_hgrn)
        yb_p, k_new, v_new = _ctx_attention(z, na_qn_g[l], na_kn_g[l], nb=bp, seq=tp, heads=na_heads, dh=dh,
                                            col0=col_nq)
        yb_s = _na_attention(z, cache_k, cache_v, bias[l], na_qn_g[l], na_kn_g[l], row_blk0=n_prompt // ts,
                             nb=bs, seq=ts, heads=na_heads, dh=dh, col0=col_nq, layer=l)
        ks_out.append(k_new)
        vs_out.append(v_new)
        ss_out.append(s_ctx)
        m = _merge((ya_p, ya_s), (yb_p, yb_s), w_hb_b[l], w_nb_b[l], z, col_ga, n_prompt, ts)
        if l % 2 == 0:
            y1, h2 = _outproj(m, w_out_b[l], y, mod[l], norm2_g[l], n_prompt, ts)
            tm = min(512, ts)
            riders = ((moe_wu[i].reshape(n_experts * d, n_ffe), nf_moe),
                      (moe_wd[i].reshape(n_experts * n_ffe, d), nn_out))
            nf_dense = ffn_wg.shape[2] // _pick(ffn_wg.shape[2], 512)
            y, cast_ud = _ffn2(h2, _tile_cols(ffn_b[0][i][None], nf_dense), _tile_cols(ffn_b[1][i][None], nf_dense),
                               _tile_cols(ffn_b[2][i][None], nn_out),
                               jnp.zeros((n // tm,), jnp.int32), jnp.full((1,), n // tm, jnp.int32),
                               tm=tm, riders=riders if cast_next else (),
                               residual=(y1, mod[l][:, 5].reshape(N_COND, 1, d), n_prompt, ts))
            if cast_next:
                moe_b = (cast_g[0].reshape(nf_moe, n_experts, d, n_ffe // nf_moe),
                         cast_ud[0].reshape(nf_moe, n_experts, d, n_ffe // nf_moe),
                         cast_ud[1].reshape(nn_out, n_experts, n_ffe, d // nn_out))
        else:
            y1, h2, gw, gi = _outproj(m, w_out_b[l], y, mod[l], norm2_g[l], n_prompt, ts, router=moe_router[i])
            pos0, pos1, tile_expert, n_tiles, fill_tiles = _routing_tables(gi[:, :2], n_experts, tm_moe,
                                                                          n_tiles_max)
            xs = _dispatch(h2, pos0, pos1, fill_tiles, n_tiles_max * tm_moe, min(1024, ts), tm_moe)
            o_sorted, _ = _ffn2(xs, *moe_b, tile_expert, n_tiles, tm=tm_moe)
            y = tuple(_combine(o_sorted, pos0, pos1, y1, gw, mod[l], n_prompt, ts))

    if not isinstance(y, tuple):
        y = (y[:n_prompt], y[n_prompt:])
    new_cache_k = jnp.concatenate(ks_out, axis=1)
    new_cache_v = jnp.concatenate(vs_out, axis=1)
    new_state_hgrn = jnp.concatenate(ss_out, axis=1)
    return (y[0].reshape(bp, tp, d), y[1].reshape(bs, ts, d), new_cache_k, new_cache_v, new_state_hgrn)
```

```python
import functools

import numpy as np
import jax
import jax.numpy as jnp
from jax import lax
from jax.experimental import pallas as pl
from jax.experimental.pallas import tpu as pltpu

F32 = jnp.float32
BF16 = jnp.bfloat16

NORM_EPS = 1e-6
NEG_INF = -1e30
GRID_W = 64
NA_QB = 16
LANES = 128
N_COND = 16
HG_BLOCK = 32
HG_HALF = HG_BLOCK // 2
VMEM_LIMIT = 56 * 1024 * 1024


def _cparams(semantics, vmem=VMEM_LIMIT):
    return pltpu.CompilerParams(dimension_semantics=semantics, vmem_limit_bytes=vmem)


def _silu(x):
    return x * jax.nn.sigmoid(x)


def _bdot(a, b):
    return jnp.dot(a.astype(BF16), b.astype(BF16), preferred_element_type=F32)


def _bdot_nt(a, b):
    return lax.dot_general(a.astype(BF16), b.astype(BF16), (((1,), (1,)), ((), ())),
                           preferred_element_type=F32)


def _bdot_tn(a, b):
    return lax.dot_general(a.astype(BF16), b.astype(BF16), (((0,), (0,)), ((), ())),
                           preferred_element_type=F32)


def _pick(n, pref):
    if n <= pref:
        return n
    t = pref - pref % LANES
    while n % t:
        t -= LANES
    return t


def _cond_of_rows(row0, n_prompt, ts):
    return jnp.where(row0 < n_prompt, 0, 1 + (row0 - n_prompt) // ts)


def _mod_body(c_ref, w_ref, b_ref, o_ref):
    o_ref[0] = _bdot(_silu(c_ref[...]), w_ref[0]) + b_ref[0]


def _modulation(cond, w_ada, b_ada):
    depth, d, n6 = w_ada.shape
    tn = _pick(n6, 1024)
    return pl.pallas_call(
        _mod_body,
        grid=(depth, n6 // tn),
        in_specs=[
            pl.BlockSpec((N_COND, d), lambda l, j: (0, 0)),
            pl.BlockSpec((1, d, tn), lambda l, j: (l, 0, j)),
            pl.BlockSpec((1, 1, tn), lambda l, j: (l, 0, j)),
        ],
        out_specs=pl.BlockSpec((1, N_COND, tn), lambda l, j: (l, 0, j)),
        out_shape=jax.ShapeDtypeStruct((depth, N_COND, n6), F32),
        compiler_params=_cparams(("parallel", "parallel")),
        name="modulation",
    )(cond, w_ada, b_ada.reshape(depth, 1, n6))


def _rms_modulate(x, g, shift, scale):
    ms = jnp.mean(x * x, axis=-1, keepdims=True)
    return (x * lax.rsqrt(ms + NORM_EPS) * g) * (1.0 + scale) + shift


def _stream_specs(y, tm, n_prompt):
    if not isinstance(y, tuple):
        return [pl.BlockSpec((tm, y.shape[1]), lambda j: (j, 0))], [y]
    d = y[0].shape[1]
    npt = n_prompt // tm
    return ([pl.BlockSpec((tm, d), lambda j: (jnp.minimum(j, npt - 1), 0)),
             pl.BlockSpec((tm, d), lambda j: (jnp.maximum(j - npt, 0), 0))], list(y))


def _stream_tile(refs, npt):
    if len(refs) == 1:
        return refs[0][...]
    return jnp.where(pl.program_id(0) < npt, refs[0][...], refs[1][...])


def _prenorm_body(*refs, npt):
    g_ref, m_ref, o_ref = refs[-3:]
    y = _stream_tile(refs[:-3], npt)
    o_ref[...] = _rms_modulate(y, g_ref[...], m_ref[0, 0:1, :], m_ref[0, 1:2, :]).astype(o_ref.dtype)


def _prenorm(y, g, mod_l, n, n_prompt, ts):
    d = g.shape[0]
    tm = min(512, ts)
    y_specs, y_args = _stream_specs(y, tm, n_prompt)
    return pl.pallas_call(
        functools.partial(_prenorm_body, npt=n_prompt // tm),
        grid=(n // tm,),
        in_specs=y_specs + [
            pl.BlockSpec((1, d), lambda j: (0, 0)),
            pl.BlockSpec((1, 6, d), lambda j: (_cond_of_rows(j * tm, n_prompt, ts), 0, 0)),
        ],
        out_specs=pl.BlockSpec((tm, d), lambda j: (j, 0)),
        out_shape=jax.ShapeDtypeStruct((n, d), BF16),
        compiler_params=_cparams(("parallel",)),
        name="prenorm",
    )(*y_args, g.reshape(1, d), mod_l)


def _rider_specs(riders, n_steps, step_of):
    in_specs, out_specs, out_shape = [], [], []
    for r in riders:
        rows, cols = r.shape
        assert rows % (n_steps * 16) == 0, (r.shape, n_steps)
        blk = (rows // n_steps, cols)
        in_specs.append(pl.BlockSpec(blk, lambda *g: (step_of(*g), 0)))
        out_specs.append(pl.BlockSpec(blk, lambda *g: (step_of(*g), 0)))
        out_shape.append(jax.ShapeDtypeStruct(r.shape, BF16))
    return in_specs, out_specs, out_shape


def _rider_copy(srcs, dsts):
    for src, dst in zip(srcs, dsts):
        dst[...] = src[...].astype(dst.dtype)


def _mm_body(x_ref, w_ref, *refs):
    n_r = (len(refs) - 1) // 2
    o_ref = refs[n_r]
    o_ref[...] = jnp.dot(x_ref[...], w_ref[...], preferred_element_type=F32).astype(o_ref.dtype)
    _rider_copy(refs[:n_r], refs[n_r + 1:])


def _matmul(x, w, riders=()):
    m, k = x.shape
    n = w.shape[1]
    tm, tn = _pick(m, 1024), _pick(n, 1536)
    gm, gn = m // tm, n // tn
    r_in, r_out, r_shape = _rider_specs(riders, gm * gn, lambda i, j: i * gn + j)
    res = pl.pallas_call(
        _mm_body,
        grid=(gm, gn),
        in_specs=[pl.BlockSpec((tm, k), lambda i, j: (i, 0)), pl.BlockSpec((k, tn), lambda i, j: (0, j))] + r_in,
        out_specs=[pl.BlockSpec((tm, tn), lambda i, j: (i, j))] + r_out,
        out_shape=[jax.ShapeDtypeStruct((m, n), F32)] + r_shape,
        compiler_params=_cparams(("parallel", "parallel")),
        name="in_proj",
    )(x, w, *riders)
    return res[0], res[1:]


def _hgrn_gates(z, lb):
    t = jnp.exp(-jnp.abs(z))
    pos = z >= 0.0
    inv = 1.0 / (1.0 + t)
    f = jnp.where(pos, 1.0 + lb * t, lb + t) * inv
    k = (1.0 - lb) * jnp.where(pos, t, 1.0) * inv
    return jnp.log(f), k


def _split2(x):
    hi = x.astype(BF16)
    lo = (x - hi.astype(F32)).astype(BF16)
    return hi, lo


def _hgrn_body(*refs, seq, hp, dh, has_s0, has_sfin, n_riders, unroll):
    zq, zff, zfb, zi, zg, lb_ref, og_ref = refs[:7]
    pos = 7
    s0_ref = None
    if has_s0:
        s0_ref = refs[pos]
        pos += 1
    r_src = refs[pos:pos + n_riders]
    pos += n_riders
    ya_ref = refs[pos]
    pos += 1
    sfin_ref = None
    if has_sfin:
        sfin_ref = refs[pos]
        pos += 1
    _rider_copy(r_src, refs[pos:pos + n_riders])
    pos += n_riders
    o_s, st_s = refs[pos:pos + 2]
    w = hp * dh
    nblk = seq // HG_BLOCK
    zf = (zff, zfb)

    for d in range(2):
        for h in range(hp):
            if has_s0:
                st_s[d * hp + h] = s0_ref[0, 0, d, h].T
            else:
                st_s[d * hp + h] = jnp.zeros((dh, dh), F32)

    row = lax.broadcasted_iota(jnp.int32, (HG_BLOCK, HG_BLOCK), 0)
    col = lax.broadcasted_iota(jnp.int32, (HG_BLOCK, HG_BLOCK), 1)
    same_half = (row // HG_HALF) == (col // HG_HALF)
    rowv = lax.broadcasted_iota(jnp.int32, (HG_BLOCK, 1), 0)
    attend = (col <= row, col >= row)
    seg = tuple(jnp.where(a & same_half, 1.0, 0.0).astype(BF16) for a in attend)
    first_half = (rowv < HG_HALF, rowv >= HG_HALF)
    first_end = (HG_HALF - 1, HG_HALF)
    second_end = (HG_BLOCK - 1, 0)

    units = [(d, h) for d in range(2) for h in range(hp)]

    def block(i, carry):
        rows = (pl.ds(pl.multiple_of(i * HG_BLOCK, HG_BLOCK), HG_BLOCK),
                pl.ds(pl.multiple_of(seq - HG_BLOCK * (i + 1), HG_BLOCK), HG_BLOCK))
        cum2, kbs = [], []
        for d in range(2):
            lf, kb = _hgrn_gates(zf[d][rows[d], :], lb_ref[d:d + 1, :])
            kbs.append(kb)
            hi, lo = _split2(lf)
            cum2.append(jnp.dot(seg[d], jnp.concatenate([hi, lo], axis=1), preferred_element_type=F32))
        vb = [zi[rows[d], :].astype(BF16) for d in range(2)]
        qe, ke, qs, kend, dec = [], [], [], [], []
        for d in range(2):
            qb = _silu(zq[rows[d], :])
            kb = kbs[d]
            cum = cum2[d][:, :w] + cum2[d][:, w:]
            l_first = cum[first_end[d]:first_end[d] + 1, :]
            l_second = cum[second_end[d]:second_end[d] + 1, :]
            l_blk = l_first + l_second
            cum_blk = cum + jnp.where(first_half[d], 0.0, l_first)
            e = cum_blk - l_first
            qe.append((qb * jnp.exp(e)).astype(BF16))
            ke.append((kb * jnp.exp(-e)).astype(BF16))
            qs.append((qb * jnp.exp(cum_blk)).astype(BF16))
            kend.append((kb * jnp.exp(l_blk - cum_blk)).astype(BF16))
            dec.append(jnp.exp(l_blk))
        sls = [slice(h * dh, (h + 1) * dh) for h in range(hp)]
        a = [_bdot_nt(qe[d][:, sls[h]], ke[d][:, sls[h]]) for d, h in units]
        st = [st_s[d * hp + h] for d, h in units]
        inter = [_bdot_nt(qs[d][:, sls[h]], st[u]) for u, (d, h) in enumerate(units)]
        ds = [_bdot_tn(vb[d][:, sls[h]], kend[d][:, sls[h]]) for d, h in units]
        intra = [_bdot(jnp.where(attend[d], a[u], 0.0), vb[d][:, sls[h]]) for u, (d, h) in enumerate(units)]
        for u, (d, h) in enumerate(units):
            st_s[d * hp + h] = st[u] * dec[d][:, sls[h]] + ds[u]
            o_s[d, rows[d], sls[h]] = intra[u] + inter[u]
        return carry

    lax.fori_loop(0, nblk, block, 0, unroll=unroll)

    def finish(c, carry):
        rows = pl.ds(pl.multiple_of(c * HG_BLOCK, HG_BLOCK), HG_BLOCK)
        o = o_s[0, rows, :] + o_s[1, rows, :]
        gate = _silu(zg[rows, :])
        for h in range(hp):
            sl = slice(h * dh, (h + 1) * dh)
            oh = o[:, sl]
            ms = jnp.mean(oh * oh, axis=-1, keepdims=True)
            ya_ref[rows, sl] = (oh * lax.rsqrt(ms + NORM_EPS) * og_ref[...] * gate[:, sl]).astype(ya_ref.dtype)
        return carry

    lax.fori_loop(0, nblk, finish, 0)
    if has_sfin:
        for d in range(2):
            for h in range(hp):
                sfin_ref[0, 0, d, h] = st_s[d * hp + h].T


def _hgrn(z, lb_l, og_l, *, row_blk0, nb, seq, heads, dh, layer, s0=None, want_state=False, riders=(), hp=None,
          unroll=2):
    if hp is None:
        hp = next(c for c in (4, 2, 1) if heads % c == 0)
    w = hp * dh
    hw = heads * dh
    ng = heads // hp
    cb = hw // w

    def zspec(k):
        return pl.BlockSpec((seq, w), lambda b, g, k=k: (row_blk0 + b, k * cb + g))

    in_specs = [zspec(0), zspec(1), zspec(2), zspec(3), zspec(4),
                pl.BlockSpec((2, w), lambda b, g: (0, g)),
                pl.BlockSpec((1, dh), lambda b, g: (0, 0))]
    args = [z, z, z, z, z, lb_l, og_l.reshape(1, dh)]
    if s0 is not None:
        in_specs.append(pl.BlockSpec((1, 1, 2, hp, dh, dh), lambda b, g: (b, layer, 0, g, 0, 0)))
        args.append(s0)
    out_specs = [pl.BlockSpec((seq, w), lambda b, g: (b, g))]
    out_shape = [jax.ShapeDtypeStruct((nb * seq, hw), BF16)]
    if want_state:
        out_specs.append(pl.BlockSpec((1, 1, 2, hp, dh, dh), lambda b, g: (b, 0, 0, g, 0, 0)))
        out_shape.append(jax.ShapeDtypeStruct((nb, 1, 2, heads, dh, dh), F32))
    r_in, r_out, r_shape = _rider_specs(riders, nb * ng, lambda b, g: b * ng + g)
    res = pl.pallas_call(
        functools.partial(_hgrn_body, seq=seq, hp=hp, dh=dh, has_s0=s0 is not None, has_sfin=want_state,
                          n_riders=len(riders), unroll=unroll),
        grid=(nb, ng),
        in_specs=in_specs + r_in,
        out_specs=out_specs + r_out,
        out_shape=out_shape + r_shape,
        scratch_shapes=[
            pltpu.VMEM((2, seq, w), F32),
            pltpu.VMEM((2 * hp, dh, dh), F32),
        ],
        compiler_params=_cparams(("parallel", "parallel")),
        name="hgrn",
    )(*args, *riders)
    n_main = len(out_shape)
    return res[:n_main], res[n_main:]


def _head_rms(x, g):
    ms = jnp.mean(x * x, axis=-1, keepdims=True)
    return x * lax.rsqrt(ms + NORM_EPS) * g


def _ctx_attn_body(zq, zk, zv, qg_ref, kg_ref, yb_ref, k_ref, v_ref, *, hp, dh):
    scale = dh ** -0.5
    for h in range(hp):
        sl = slice(h * dh, (h + 1) * dh)
        q = _head_rms(zq[:, sl], qg_ref[...])
        k = _head_rms(zk[:, sl], kg_ref[...])
        v = zv[:, sl]
        k_ref[0, 0, h] = k
        v_ref[0, 0, h] = v
        s = _bdot_nt(q, k) * scale
        p = jnp.exp(s - jnp.max(s, axis=-1, keepdims=True))
        o = _bdot(p, v) / jnp.sum(p, axis=-1, keepdims=True)
        yb_ref[:, sl] = o.astype(yb_ref.dtype)


def _ctx_attention(z, qg, kg, *, nb, seq, heads, dh, col0):
    hp = 2 if heads % 2 == 0 else 1
    w = hp * dh
    nw = heads * dh
    ng = heads // hp
    c0 = col0 // w
    cb = nw // w

    def zspec(k):
        return pl.BlockSpec((seq, w), lambda b, g, k=k: (b, c0 + k * cb + g))

    kv_spec = pl.BlockSpec((1, 1, hp, seq, dh), lambda b, g: (b, 0, g, 0, 0))
    kv_shape = jax.ShapeDtypeStruct((nb, 1, heads, seq, dh), F32)
    return pl.pallas_call(
        functools.partial(_ctx_attn_body, hp=hp, dh=dh),
        grid=(nb, ng),
        in_specs=[zspec(0), zspec(1), zspec(2),
                  pl.BlockSpec((1, dh), lambda b, g: (0, 0)), pl.BlockSpec((1, dh), lambda b, g: (0, 0))],
        out_specs=[pl.BlockSpec((seq, w), lambda b, g: (b, g)), kv_spec, kv_spec],
        out_shape=[jax.ShapeDtypeStruct((nb * seq, nw), BF16), kv_shape, kv_shape],
        compiler_params=_cparams(("parallel", "parallel")),
        name="ctx_attention",
    )(z, z, z, qg.reshape(1, dh), kg.reshape(1, dh))


def _na_bias(rpb, seq):
    win_r, win_c = (rpb.shape[2] + 1) // 2, (rpb.shape[3] + 1) // 2
    rows = seq // GRID_W
    wr = min(win_r, rows)
    r, c = np.arange(rows), np.arange(GRID_W)
    r0 = np.clip(r - wr // 2, 0, rows - wr)
    c0 = np.clip(c - win_c // 2, 0, GRID_W - win_c)
    row_ok = (r[None, :] >= r0[:, None]) & (r[None, :] < r0[:, None] + wr)
    col_ok = (c[None, :] >= c0[:, None]) & (c[None, :] < c0[:, None] + win_c)
    valid = (row_ok[:, None, :, None] & col_ok[None, :, None, :]).reshape(seq, seq)
    sel_r = (r[None, :, None] - r[:, None, None] + win_r - 1 == np.arange(2 * win_r - 1)).astype(np.float32)
    sel_c = (c[None, :, None] - c[:, None, None] + win_c - 1 == np.arange(2 * win_c - 1)).astype(np.float32)
    table = jnp.einsum("xka,lhab,cqb->lhxckq", sel_r, rpb.astype(F32), sel_c, precision=lax.Precision.HIGHEST)
    table = table.reshape(rpb.shape[0], rpb.shape[1], seq, seq)
    return jnp.where(valid[None, None], table, NEG_INF)


def _na_attn_body(zq, zk, zv, kc_ref, vc_ref, bias_ref, qg_ref, kg_ref, yb_ref, *, seq, dh, tq):
    scale = dh ** -0.5
    q = _head_rms(zq[...], qg_ref[...]).astype(BF16)
    k = _head_rms(zk[...], kg_ref[...]).astype(BF16)
    v = zv[...].astype(BF16)
    kc = kc_ref[0, 0, 0].astype(BF16)
    vc = vc_ref[0, 0, 0].astype(BF16)
    for i in range(seq // tq):
        rows = slice(i * tq, (i + 1) * tq)
        qt = q[rows]
        s_win = _bdot_nt(qt, k) * scale + bias_ref[0, rows, :]
        s_ctx = _bdot_nt(qt, kc) * scale
        m = jnp.maximum(jnp.max(s_win, axis=-1, keepdims=True), jnp.max(s_ctx, axis=-1, keepdims=True))
        p_win = jnp.exp(s_win - m)
        p_ctx = jnp.exp(s_ctx - m)
        den = jnp.sum(p_win, axis=-1, keepdims=True) + jnp.sum(p_ctx, axis=-1, keepdims=True)
        o = (_bdot(p_win, v) + _bdot(p_ctx, vc)) / den
        yb_ref[rows, :] = o.astype(yb_ref.dtype)


def _na_attention(z, cache_k, cache_v, bias, qg, kg, *, row_blk0, nb, seq, heads, dh, col0, layer):
    nw = heads * dh
    c0 = col0 // dh
    past = cache_k.shape[3]

    def zspec(k):
        return pl.BlockSpec((seq, dh), lambda h, b, k=k: (row_blk0 + b, c0 + k * heads + h))

    cache_spec = pl.BlockSpec((1, 1, 1, past, dh), lambda h, b: (b, layer, h, 0, 0))
    return pl.pallas_call(
        functools.partial(_na_attn_body, seq=seq, dh=dh, tq=min(256, seq)),
        grid=(heads, nb),
        in_specs=[zspec(0), zspec(1), zspec(2), cache_spec, cache_spec,
                  pl.BlockSpec((1, seq, seq), lambda h, b: (h, 0, 0)),
                  pl.BlockSpec((1, dh), lambda h, b: (0, 0)), pl.BlockSpec((1, dh), lambda h, b: (0, 0))],
        out_specs=pl.BlockSpec((seq, dh), lambda h, b: (b, h)),
        out_shape=jax.ShapeDtypeStruct((nb * seq, nw), BF16),
        compiler_params=_cparams(("parallel", "parallel")),
        name="na_attention",
    )(z, z, z, cache_k, cache_v, bias, qg.reshape(1, dh), kg.reshape(1, dh))


def _merge_body(yap_ref, yas_ref, ybp_ref, ybs_ref, wh_ref, wn_ref, ga_ref, gb_ref, o_ref, *, npt):
    is_prompt = pl.program_id(0) < npt
    ya = jnp.where(is_prompt, yap_ref[...], yas_ref[...])
    yb = jnp.where(is_prompt, ybp_ref[...], ybs_ref[...])
    a = jnp.dot(ya, wh_ref[...], preferred_element_type=F32)
    b = jnp.dot(yb, wn_ref[...], preferred_element_type=F32)
    o_ref[...] = (jax.nn.sigmoid(ga_ref[...]) * a + jax.nn.sigmoid(gb_ref[...]) * b).astype(o_ref.dtype)


def _merge(ya, yb, w_hb, w_nb, z, col_ga, n_prompt, ts):
    n = z.shape[0]
    hw, nw = ya[0].shape[1], yb[0].shape[1]
    d = w_hb.shape[1]
    tm, tn = min(512, ts), _pick(d, 1024)
    npt = n_prompt // tm
    ca, cbk = col_ga // tn, (col_ga + d) // tn

    def split(width):
        return [pl.BlockSpec((tm, width), lambda i, j: (jnp.minimum(i, npt - 1), 0)),
                pl.BlockSpec((tm, width), lambda i, j: (jnp.maximum(i - npt, 0), 0))]

    return pl.pallas_call(
        functools.partial(_merge_body, npt=npt),
        grid=(n // tm, d // tn),
        in_specs=split(hw) + split(nw) + [
            pl.BlockSpec((hw, tn), lambda i, j: (0, j)),
            pl.BlockSpec((nw, tn), lambda i, j: (0, j)),
            pl.BlockSpec((tm, tn), lambda i, j: (i, ca + j)),
            pl.BlockSpec((tm, tn), lambda i, j: (i, cbk + j)),
        ],
        out_specs=pl.BlockSpec((tm, tn), lambda i, j: (i, j)),
        out_shape=jax.ShapeDtypeStruct((n, d), BF16),
        compiler_params=_cparams(("parallel", "parallel")),
        name="merge",
    )(*ya, *yb, w_hb, w_nb, z, z)


def _outproj_body(*refs, n_experts, n_stream, npt):
    y = _stream_tile(refs[:n_stream], npt)
    m_ref, w_ref, mod_ref, g_ref = refs[n_stream:n_stream + 4]
    if n_experts:
        r_ref, y1_ref, h2_ref, gw_ref, gi_ref = refs[n_stream + 4:]
    else:
        y1_ref, h2_ref = refs[n_stream + 4:]
    y1 = y + mod_ref[0, 2:3, :] * jnp.dot(m_ref[...], w_ref[...], preferred_element_type=F32)
    y1_ref[...] = y1
    h2 = _rms_modulate(y1, g_ref[...], mod_ref[0, 3:4, :], mod_ref[0, 4:5, :])
    h2_ref[...] = h2.astype(h2_ref.dtype)
    if n_experts:
        r = r_ref[...]
        h_hi = h2.astype(BF16)
        h_lo = (h2 - h_hi.astype(F32)).astype(BF16)
        r_hi = r.astype(BF16)
        r_lo = (r - r_hi.astype(F32)).astype(BF16)
        logits = (jnp.dot(h_hi, r_hi, preferred_element_type=F32)
                  + jnp.dot(h_hi, r_lo, preferred_element_type=F32)
                  + jnp.dot(h_lo, r_hi, preferred_element_type=F32))
        lane = lax.broadcasted_iota(jnp.int32, logits.shape, 1).astype(F32)
        big = float(LANES)
        lg = jnp.where(lane < n_experts, logits, -jnp.inf)
        m1 = jnp.max(lg, axis=-1, keepdims=True)
        i1 = jnp.min(jnp.where(lg == m1, lane, big), axis=-1, keepdims=True)
        lg2 = jnp.where(lane == i1, -jnp.inf, lg)
        m2 = jnp.max(lg2, axis=-1, keepdims=True)
        i2 = jnp.min(jnp.where(lg2 == m2, lane, big), axis=-1, keepdims=True)
        t = jnp.exp(m2 - m1)
        w1 = 1.0 / (1.0 + t)
        w2 = t / (1.0 + t)
        gw_ref[...] = jnp.where(lane == 0.0, w1, jnp.where(lane == 1.0, w2, 0.0))
        gi_ref[...] = jnp.where(lane == 0.0, i1, jnp.where(lane == 1.0, i2, 0.0)).astype(jnp.int32)


def _outproj(m, w_out, y, mod_l, g2, n_prompt, ts, router=None):
    n, d = m.shape
    tm = min(256, ts)
    n_experts = 0 if router is None else router.shape[1]
    y_specs, y_args = _stream_specs(y, tm, n_prompt)
    in_specs = y_specs + [
        pl.BlockSpec((tm, d), lambda j: (j, 0)),
        pl.BlockSpec((d, d), lambda j: (0, 0)),
        pl.BlockSpec((1, 6, d), lambda j: (_cond_of_rows(j * tm, n_prompt, ts), 0, 0)),
        pl.BlockSpec((1, d), lambda j: (0, 0)),
    ]
    args = y_args + [m, w_out, mod_l, g2.reshape(1, d)]
    row_spec = pl.BlockSpec((tm, d), lambda j: (j, 0))
    out_specs = [row_spec, row_spec]
    out_shape = [jax.ShapeDtypeStruct((n, d), F32), jax.ShapeDtypeStruct((n, d), F32 if n_experts else BF16)]
    if n_experts:
        in_specs.append(pl.BlockSpec((d, LANES), lambda j: (0, 0)))
        args.append(jnp.pad(router, ((0, 0), (0, LANES - n_experts))))
        lane_spec = pl.BlockSpec((tm, LANES), lambda j: (j, 0))
        out_specs += [lane_spec, lane_spec]
        out_shape += [jax.ShapeDtypeStruct((n, LANES), F32), jax.ShapeDtypeStruct((n, LANES), jnp.int32)]
    return pl.pallas_call(
        functools.partial(_outproj_body, n_experts=n_experts, n_stream=len(y_args), npt=n_prompt // tm),
        grid=(n // tm,),
        in_specs=in_specs,
        out_specs=out_specs,
        out_shape=out_shape,
        compiler_params=_cparams(("parallel",)),
        name="outproj",
    )(*args)


def _ffn_body(te_ref, nt_ref, *refs, residual):
    del te_ref
    if residual:
        x_ref, wg_ref, wu_ref, wd_ref, y_ref, mod_ref, o_ref = refs
    else:
        x_ref, wg_ref, wu_ref, wd_ref, o_ref = refs
    j, f = pl.program_id(0), pl.program_id(1)
    nf = pl.num_programs(1)

    @pl.when(j < nt_ref[0])
    def _():
        x = x_ref[...].astype(BF16)
        g = jnp.dot(x, wg_ref[0], preferred_element_type=F32)
        u = jnp.dot(x, wu_ref[0], preferred_element_type=F32)
        part = jnp.dot((_silu(g) * u).astype(BF16), wd_ref[0], preferred_element_type=F32)

        @pl.when(f == 0)
        def _():
            o_ref[...] = part

        @pl.when(f > 0)
        def _():
            o_ref[...] += part

        if residual:
            @pl.when(f == nf - 1)
            def _():
                o_ref[...] = y_ref[...] + mod_ref[0, 5:6, :] * o_ref[...]

    @pl.when((j >= nt_ref[0]) & (f == 0))
    def _():
        o_ref[...] = jnp.zeros_like(o_ref)


def _ffn(x, wg, wu, wd, tile_expert, n_tiles, *, tm, tf, residual=None):
    n, d = x.shape
    ff = wg.shape[2]
    tf = _pick(ff, tf)
    nf = ff // tf

    def row_map(j, f, te, nt):
        return (jnp.minimum(j, nt[0] - 1), 0)

    def ff_idx(j, f, nt):
        return jnp.where(j < nt[0], f, nf - 1)

    def e_idx(j, te, nt):
        return te[jnp.minimum(j, nt[0] - 1)]

    in_specs = [
        pl.BlockSpec((tm, d), row_map),
        pl.BlockSpec((1, d, tf), lambda j, f, te, nt: (e_idx(j, te, nt), 0, ff_idx(j, f, nt))),
        pl.BlockSpec((1, d, tf), lambda j, f, te, nt: (e_idx(j, te, nt), 0, ff_idx(j, f, nt))),
        pl.BlockSpec((1, tf, d), lambda j, f, te, nt: (e_idx(j, te, nt), ff_idx(j, f, nt), 0)),
    ]
    args = [x, wg, wu, wd]
    if residual is not None:
        y, mod_l, n_prompt, ts = residual
        in_specs += [
            pl.BlockSpec((tm, d), row_map),
            pl.BlockSpec((1, 6, d), lambda j, f, te, nt: (_cond_of_rows(j * tm, n_prompt, ts), 0, 0)),
        ]
        args += [y, mod_l]
    return pl.pallas_call(
        functools.partial(_ffn_body, residual=residual is not None),
        grid_spec=pltpu.PrefetchScalarGridSpec(
            num_scalar_prefetch=2,
            grid=(n // tm, nf),
            in_specs=in_specs,
            out_specs=pl.BlockSpec((tm, d), lambda j, f, te, nt: (j, 0)),
        ),
        out_shape=jax.ShapeDtypeStruct((n, d), F32),
        compiler_params=_cparams(("arbitrary", "arbitrary")),
        name="swiglu",
    )(tile_expert, n_tiles, *args)


def _dispatch_body(fill_ref, p0_ref, p1_ref, src, dst, zero_s, sem, zsem, *, tb, tm):
    @pl.when(pl.program_id(0) == 0)
    def _():
        zero_s[...] = jnp.zeros_like(zero_s)

        def fill(u):
            row0 = pl.multiple_of(jnp.maximum(fill_ref[u], 0) * tm, tm)
            return pltpu.make_async_copy(zero_s, dst.at[pl.ds(row0, tm)], zsem)

        for u in range(fill_ref.shape[0]):
            pl.when(fill_ref[u] >= 0)(fill(u).start)
        for u in range(fill_ref.shape[0]):
            pl.when(fill_ref[u] >= 0)(fill(u).wait)

    def issue(t, c):
        row = src.at[pl.ds(t, 1)]
        pltpu.make_async_copy(row, dst.at[pl.ds(p0_ref[0, 0, t], 1)], sem).start()
        pltpu.make_async_copy(row, dst.at[pl.ds(p1_ref[0, 0, t], 1)], sem).start()
        return c

    lax.fori_loop(0, tb, issue, 0)
    for _ in range(2):
        pltpu.make_async_copy(src, dst.at[pl.ds(0, tb)], sem).wait()


def _dispatch(h, pos0, pos1, fill_tiles, n_sorted, tb, tm):
    n, d = h.shape
    return pl.pallas_call(
        functools.partial(_dispatch_body, tb=tb, tm=tm),
        grid=(n // tb,),
        in_specs=[
            pl.BlockSpec(memory_space=pltpu.SMEM),
            pl.BlockSpec((1, 1, tb), lambda i: (i, 0, 0), memory_space=pltpu.SMEM),
            pl.BlockSpec((1, 1, tb), lambda i: (i, 0, 0), memory_space=pltpu.SMEM),
            pl.BlockSpec((tb, d), lambda i: (i, 0)),
        ],
        out_specs=pl.BlockSpec(memory_space=pl.ANY),
        out_shape=jax.ShapeDtypeStruct((n_sorted, d), h.dtype),
        scratch_shapes=[pltpu.VMEM((tm, d), h.dtype), pltpu.SemaphoreType.DMA(()), pltpu.SemaphoreType.DMA(())],
        compiler_params=_cparams(("arbitrary",)),
        name="dispatch",
    )(fill_tiles, pos0.reshape(n // tb, 1, tb), pos1.reshape(n // tb, 1, tb), h)


def _combine_body(p0_ref, p1_ref, o_hbm, y_ref, gw_ref, mod_ref, outp_ref, outs_ref, buf0, buf1, sem, *, tc, npt):
    def issue(t, c):
        pltpu.make_async_copy(o_hbm.at[pl.ds(p0_ref[0, 0, t], 1)], buf0.at[pl.ds(t, 1)], sem).start()
        pltpu.make_async_copy(o_hbm.at[pl.ds(p1_ref[0, 0, t], 1)], buf1.at[pl.ds(t, 1)], sem).start()
        return c

    lax.fori_loop(0, tc, issue, 0)
    pltpu.make_async_copy(o_hbm.at[pl.ds(0, tc)], buf0, sem).wait()
    pltpu.make_async_copy(o_hbm.at[pl.ds(0, tc)], buf1, sem).wait()
    gw = gw_ref[...]
    mix = gw[:, 0:1] * buf0[...] + gw[:, 1:2] * buf1[...]
    res = y_ref[...] + mod_ref[0, 5:6, :] * mix
    i = pl.program_id(0)

    @pl.when(i < npt)
    def _():
        outp_ref[...] = res

    @pl.when(i >= npt)
    def _():
        outs_ref[...] = res


def _combine(o_sorted, pos0, pos1, y, gw, mod_l, n_prompt, ts):
    n, d = y.shape
    tc = min(256, ts)
    npt = n_prompt // tc
    pos0, pos1 = pos0.reshape(n // tc, 1, tc), pos1.reshape(n // tc, 1, tc)
    return pl.pallas_call(
        functools.partial(_combine_body, tc=tc, npt=npt),
        grid=(n // tc,),
        in_specs=[
            pl.BlockSpec((1, 1, tc), lambda i: (i, 0, 0), memory_space=pltpu.SMEM),
            pl.BlockSpec((1, 1, tc), lambda i: (i, 0, 0), memory_space=pltpu.SMEM),
            pl.BlockSpec(memory_space=pl.ANY),
            pl.BlockSpec((tc, d), lambda i: (i, 0)),
            pl.BlockSpec((tc, LANES), lambda i: (i, 0)),
            pl.BlockSpec((1, 6, d), lambda i: (_cond_of_rows(i * tc, n_prompt, ts), 0, 0)),
        ],
        out_specs=[pl.BlockSpec((tc, d), lambda i: (jnp.minimum(i, npt - 1), 0)),
                   pl.BlockSpec((tc, d), lambda i: (jnp.maximum(i - npt, 0), 0))],
        out_shape=[jax.ShapeDtypeStruct((n_prompt, d), F32), jax.ShapeDtypeStruct((n - n_prompt, d), F32)],
        scratch_shapes=[pltpu.VMEM((tc, d), F32), pltpu.VMEM((tc, d), F32), pltpu.SemaphoreType.DMA(())],
        compiler_params=_cparams(("arbitrary",)),
        name="combine",
    )(pos0, pos1, o_sorted, y, gw, mod_l)


def _routing_tables(top_i, n_experts, tm, n_tiles_max):
    n = top_i.shape[0]
    flat = jnp.concatenate([top_i[:, 0], top_i[:, 1]])
    onehot = (flat[:, None] == jnp.arange(n_experts, dtype=jnp.int32)[None, :]).astype(jnp.int32)
    csum = jnp.cumsum(onehot, axis=0)
    rank = jnp.sum((csum - onehot) * onehot, axis=1)
    counts = csum[-1]
    tiles = (counts + tm - 1) // tm
    tile_end = jnp.cumsum(tiles)
    start = (tile_end - tiles) * tm
    pos = jnp.sum(onehot * start[None, :], axis=1) + rank
    n_tiles = tile_end[-1]
    j = jnp.arange(n_tiles_max, dtype=jnp.int32)
    tile_expert = jnp.sum((j[:, None] >= tile_end[None, :]).astype(jnp.int32), axis=1)
    tile_expert = jnp.minimum(tile_expert, n_experts - 1).astype(jnp.int32)
    group_last = jnp.where(tiles > 0, tile_end - 1, -1)
    tail = n_tiles + jnp.arange(n_experts, dtype=jnp.int32)
    tail = jnp.where(tail < n_tiles_max, tail, -1)
    fill_tiles = jnp.concatenate([group_last, tail]).astype(jnp.int32)
    return (pos[:n].astype(jnp.int32), pos[n:].astype(jnp.int32), tile_expert,
            n_tiles.reshape(1).astype(jnp.int32), fill_tiles)


def kernel(x_prompt, x_sample, cache_k, cache_v, state_hgrn, c, c_ctx, norm1_g, norm2_g, w_ada, b_ada, w_in,
           hg_lb, hg_onorm_g, na_qn_g, na_kn_g, na_rpb, w_hb, w_nb, w_out, ffn_wg, ffn_wu, ffn_wd,
           moe_router, moe_wg, moe_wu, moe_wd):
    bp, tp, d = x_prompt.shape
    bs, ts, _ = x_sample.shape
    depth = w_in.shape[0]
    hg_heads, dh = state_hgrn.shape[3], state_hgrn.shape[4]
    na_heads = cache_k.shape[2]
    hw, nw = hg_heads * dh, na_heads * dh
    n_experts = moe_wg.shape[1]
    n_prompt, n_sample = bp * tp, bs * ts
    n = n_prompt + n_sample
    assert n_prompt % ts == 0 and ts % tp == 0 and bs + 1 <= N_COND
    col_nq = 5 * hw
    col_ga = 5 * hw + 3 * nw

    p = jax.nn.softmax(hg_lb.astype(F32), axis=1)
    cs = jnp.cumsum(p, axis=1)
    lbs = cs - cs[:, :1]
    bias = _na_bias(na_rpb, ts)
    w_in_b, w_hb_b, w_nb_b, w_out_b = (a.astype(BF16) for a in (w_in, w_hb, w_nb, w_out))
    ffn_b = tuple(a.astype(BF16) for a in (ffn_wg, ffn_wu, ffn_wd))
    n_ffe = moe_wg.shape[3]

    cond = jnp.concatenate([c_ctx[None, :], c, jnp.zeros((N_COND - 1 - bs, d), F32)], axis=0)
    mod = _modulation(cond, w_ada, b_ada).reshape(depth, N_COND, 6, d)

    y = (x_prompt.reshape(n_prompt, d), x_sample.reshape(n_sample, d))
    tm_moe = min(512, n)
    n_tiles_max = (2 * n) // tm_moe + n_experts
    ks_out, vs_out, ss_out = [], [], []
    moe_b = {}
    for l in range(depth):
        i = l // 2
        cast_next = l % 2 == 0 and l + 1 < depth
        if cast_next:
            in_riders = (moe_wg[i].reshape(n_experts * d, n_ffe),)
            hg_riders = (moe_wd[i].reshape(n_experts * n_ffe, d),)
        elif l % 2 == 1:
            in_riders, hg_riders = (moe_wu[i].reshape(n_experts * d, n_ffe),), ()
        else:
            in_riders, hg_riders = (), ()
        h = _prenorm(y, norm1_g[l], mod[l], n, n_prompt, ts)
        z, in_cast = _matmul(h, w_in_b[l], in_riders)
        (ya_p, s_ctx), hg_cast = _hgrn(z, lbs[:, l], hg_onorm_g[l], row_blk0=0, nb=bp, seq=tp, heads=hg_heads,
                                       dh=dh, layer=l, want_state=True, riders=hg_riders)
        (ya_s,), _ = _hgrn(z, lbs[:, l], hg_onorm_g[l], row_blk0=n_prompt // ts, nb=bs, seq=ts, heads=hg_heads,
                           dh=dh, layer=l, s0=state_hgrn)
        if cast_next:
            moe_b["g"] = in_cast[0].reshape(n_experts, d, n_ffe)
            moe_b["d"] = hg_cast[0].reshape(n_experts, n_ffe, d)
        elif l % 2 == 1:
            moe_b["u"] = in_cast[0].reshape(n_experts, d, n_ffe)
        yb_p, k_new, v_new = _ctx_attention(z, na_qn_g[l], na_kn_g[l], nb=bp, seq=tp, heads=na_heads, dh=dh,
                                            col0=col_nq)
        yb_s = _na_attention(z, cache_k, cache_v, bias[l], na_qn_g[l], na_kn_g[l], row_blk0=n_prompt // ts,
                             nb=bs, seq=ts, heads=na_heads, dh=dh, col0=col_nq, layer=l)
        ks_out.append(k_new)
        vs_out.append(v_new)
        ss_out.append(s_ctx)
        m = _merge((ya_p, ya_s), (yb_p, yb_s), w_hb_b[l], w_nb_b[l], z, col_ga, n_prompt, ts)
        if l % 2 == 0:
            y1, h2 = _outproj(m, w_out_b[l], y, mod[l], norm2_g[l], n_prompt, ts)
            tm = min(512, ts)
            y = _ffn(h2, ffn_b[0][i][None], ffn_b[1][i][None], ffn_b[2][i][None],
                     jnp.zeros((n // tm,), jnp.int32), jnp.full((1,), n // tm, jnp.int32), tm=tm, tf=512,
                     residual=(y1, mod[l], n_prompt, ts))
        else:
            y1, h2, gw, gi = _outproj(m, w_out_b[l], y, mod[l], norm2_g[l], n_prompt, ts, router=moe_router[i])
            pos0, pos1, tile_expert, n_tiles, fill_tiles = _routing_tables(gi[:, :2], n_experts, tm_moe,
                                                                          n_tiles_max)
            xs = _dispatch(h2, pos0, pos1, fill_tiles, n_tiles_max * tm_moe, min(1024, ts), tm_moe)
            o_sorted = _ffn(xs, moe_b["g"], moe_b["u"], moe_b["d"], tile_expert, n_tiles, tm=tm_moe, tf=1024)
            y = tuple(_combine(o_sorted, pos0, pos1, y1, gw, mod[l], n_prompt, ts))

    if not isinstance(y, tuple):
        y = (y[:n_prompt], y[n_prompt:])
    new_cache_k = jnp.concatenate(ks_out, axis=1)
    new_cache_v = jnp.concatenate(vs_out, axis=1)
    new_state_hgrn = jnp.concatenate(ss_out, axis=1)
    return (y[0].reshape(bp, tp, d), y[1].reshape(bs, ts, d), new_cache_k, new_cache_v, new_state_hgrn)
```

```python
import functools

import numpy as np
import jax
import jax.numpy as jnp
from jax import lax
from jax.experimental import pallas as pl
from jax.experimental.pallas import tpu as pltpu

F32 = jnp.float32
BF16 = jnp.bfloat16

NORM_EPS = 1e-6
NEG_INF = -1e30
GRID_W = 64
NA_QB = 16
LANES = 128
N_COND = 16
HG_BLOCK = 32
HG_HALF = HG_BLOCK // 2
VMEM_LIMIT = 56 * 1024 * 1024


def _cparams(semantics, vmem=VMEM_LIMIT):
    return pltpu.CompilerParams(dimension_semantics=semantics, vmem_limit_bytes=vmem)


def _silu(x):
    return x * jax.nn.sigmoid(x)


def _bdot(a, b):
    return jnp.dot(a.astype(BF16), b.astype(BF16), preferred_element_type=F32)


def _bdot_nt(a, b):
    return lax.dot_general(a.astype(BF16), b.astype(BF16), (((1,), (1,)), ((), ())),
                           preferred_element_type=F32)


def _bdot_tn(a, b):
    return lax.dot_general(a.astype(BF16), b.astype(BF16), (((0,), (0,)), ((), ())),
                           preferred_element_type=F32)


def _pick(n, pref):
    if n <= pref:
        return n
    t = pref - pref % LANES
    while n % t:
        t -= LANES
    return t


def _cond_of_rows(row0, n_prompt, ts):
    return jnp.where(row0 < n_prompt, 0, 1 + (row0 - n_prompt) // ts)


def _mod_body(c_ref, w_ref, b_ref, o_ref):
    o_ref[0] = _bdot(_silu(c_ref[...]), w_ref[0]) + b_ref[0]


def _modulation(cond, w_ada, b_ada):
    depth, d, n6 = w_ada.shape
    tn = _pick(n6, 1024)
    return pl.pallas_call(
        _mod_body,
        grid=(depth, n6 // tn),
        in_specs=[
            pl.BlockSpec((N_COND, d), lambda l, j: (0, 0)),
            pl.BlockSpec((1, d, tn), lambda l, j: (l, 0, j)),
            pl.BlockSpec((1, 1, tn), lambda l, j: (l, 0, j)),
        ],
        out_specs=pl.BlockSpec((1, N_COND, tn), lambda l, j: (l, 0, j)),
        out_shape=jax.ShapeDtypeStruct((depth, N_COND, n6), F32),
        compiler_params=_cparams(("parallel", "parallel")),
        name="modulation",
    )(cond, w_ada, b_ada.reshape(depth, 1, n6))


def _rms_modulate(x, g, shift, scale):
    ms = jnp.mean(x * x, axis=-1, keepdims=True)
    return (x * lax.rsqrt(ms + NORM_EPS) * g) * (1.0 + scale) + shift


def _stream_specs(y, tm, n_prompt):
    if not isinstance(y, tuple):
        return [pl.BlockSpec((tm, y.shape[1]), lambda j: (j, 0))], [y]
    d = y[0].shape[1]
    npt = n_prompt // tm
    return ([pl.BlockSpec((tm, d), lambda j: (jnp.minimum(j, npt - 1), 0)),
             pl.BlockSpec((tm, d), lambda j: (jnp.maximum(j - npt, 0), 0))], list(y))


def _stream_tile(refs, npt):
    if len(refs) == 1:
        return refs[0][...]
    return jnp.where(pl.program_id(0) < npt, refs[0][...], refs[1][...])


def _prenorm_body(*refs, npt):
    g_ref, m_ref, o_ref = refs[-3:]
    y = _stream_tile(refs[:-3], npt)
    o_ref[...] = _rms_modulate(y, g_ref[...], m_ref[0, 0:1, :], m_ref[0, 1:2, :]).astype(o_ref.dtype)


def _prenorm(y, g, mod_l, n, n_prompt, ts):
    d = g.shape[0]
    tm = min(512, ts)
    y_specs, y_args = _stream_specs(y, tm, n_prompt)
    return pl.pallas_call(
        functools.partial(_prenorm_body, npt=n_prompt // tm),
        grid=(n // tm,),
        in_specs=y_specs + [
            pl.BlockSpec((1, d), lambda j: (0, 0)),
            pl.BlockSpec((1, 6, d), lambda j: (_cond_of_rows(j * tm, n_prompt, ts), 0, 0)),
        ],
        out_specs=pl.BlockSpec((tm, d), lambda j: (j, 0)),
        out_shape=jax.ShapeDtypeStruct((n, d), BF16),
        compiler_params=_cparams(("parallel",)),
        name="prenorm",
    )(*y_args, g.reshape(1, d), mod_l)


def _rider_specs(riders, n_steps, step_of):
    in_specs, out_specs, out_shape = [], [], []
    for r in riders:
        rows, cols = r.shape
        assert rows % (n_steps * 16) == 0, (r.shape, n_steps)
        blk = (rows // n_steps, cols)
        in_specs.append(pl.BlockSpec(blk, lambda *g: (step_of(*g), 0)))
        out_specs.append(pl.BlockSpec(blk, lambda *g: (step_of(*g), 0)))
        out_shape.append(jax.ShapeDtypeStruct(r.shape, BF16))
    return in_specs, out_specs, out_shape


def _rider_copy(srcs, dsts):
    for src, dst in zip(srcs, dsts):
        dst[...] = src[...].astype(dst.dtype)


def _mm_body(x_ref, w_ref, *refs):
    n_r = (len(refs) - 1) // 2
    o_ref = refs[n_r]
    o_ref[...] = jnp.dot(x_ref[...], w_ref[...], preferred_element_type=F32).astype(o_ref.dtype)
    _rider_copy(refs[:n_r], refs[n_r + 1:])


def _matmul(x, w, riders=()):
    m, k = x.shape
    n = w.shape[1]
    tm, tn = _pick(m, 1024), _pick(n, 1536)
    gm, gn = m // tm, n // tn
    r_in, r_out, r_shape = _rider_specs(riders, gm * gn, lambda i, j: i * gn + j)
    res = pl.pallas_call(
        _mm_body,
        grid=(gm, gn),
        in_specs=[pl.BlockSpec((tm, k), lambda i, j: (i, 0)), pl.BlockSpec((k, tn), lambda i, j: (0, j))] + r_in,
        out_specs=[pl.BlockSpec((tm, tn), lambda i, j: (i, j))] + r_out,
        out_shape=[jax.ShapeDtypeStruct((m, n), F32)] + r_shape,
        compiler_params=_cparams(("parallel", "parallel")),
        name="in_proj",
    )(x, w, *riders)
    return res[0], res[1:]


def _hgrn_gates(z, lb):
    t = jnp.exp(-jnp.abs(z))
    pos = z >= 0.0
    inv = 1.0 / (1.0 + t)
    f = jnp.where(pos, 1.0 + lb * t, lb + t) * inv
    k = (1.0 - lb) * jnp.where(pos, t, 1.0) * inv
    return jnp.log(f), k


def _split2(x):
    hi = x.astype(BF16)
    lo = (x - hi.astype(F32)).astype(BF16)
    return hi, lo


def _hgrn_body(*refs, seq, hp, dh, has_s0, has_sfin, n_riders, unroll):
    zq, zff, zfb, zi, zg, lb_ref, og_ref = refs[:7]
    pos = 7
    s0_ref = None
    if has_s0:
        s0_ref = refs[pos]
        pos += 1
    r_src = refs[pos:pos + n_riders]
    pos += n_riders
    ya_ref = refs[pos]
    pos += 1
    sfin_ref = None
    if has_sfin:
        sfin_ref = refs[pos]
        pos += 1
    _rider_copy(r_src, refs[pos:pos + n_riders])
    pos += n_riders
    o_s, st_s = refs[pos:pos + 2]
    w = hp * dh
    nblk = seq // HG_BLOCK
    zf = (zff, zfb)

    for d in range(2):
        for h in range(hp):
            if has_s0:
                st_s[d * hp + h] = s0_ref[0, 0, d, h].T
            else:
                st_s[d * hp + h] = jnp.zeros((dh, dh), F32)

    row = lax.broadcasted_iota(jnp.int32, (HG_BLOCK, HG_BLOCK), 0)
    col = lax.broadcasted_iota(jnp.int32, (HG_BLOCK, HG_BLOCK), 1)
    same_half = (row // HG_HALF) == (col // HG_HALF)
    rowv = lax.broadcasted_iota(jnp.int32, (HG_BLOCK, 1), 0)
    attend = (col <= row, col >= row)
    seg = tuple(jnp.where(a & same_half, 1.0, 0.0).astype(BF16) for a in attend)
    first_half = (rowv < HG_HALF, rowv >= HG_HALF)
    first_end = (HG_HALF - 1, HG_HALF)
    second_end = (HG_BLOCK - 1, 0)

    units = [(d, h) for d in range(2) for h in range(hp)]

    def block(i, carry):
        rows = (pl.ds(pl.multiple_of(i * HG_BLOCK, HG_BLOCK), HG_BLOCK),
                pl.ds(pl.multiple_of(seq - HG_BLOCK * (i + 1), HG_BLOCK), HG_BLOCK))
        cum2, kbs = [], []
        for d in range(2):
            lf, kb = _hgrn_gates(zf[d][rows[d], :], lb_ref[d:d + 1, :])
            kbs.append(kb)
            hi, lo = _split2(lf)
            cum2.append(jnp.dot(seg[d], jnp.concatenate([hi, lo], axis=1), preferred_element_type=F32))
        vb = [zi[rows[d], :].astype(BF16) for d in range(2)]
        qe, ke, qs, kend, dec = [], [], [], [], []
        for d in range(2):
            qb = _silu(zq[rows[d], :])
            kb = kbs[d]
            cum = cum2[d][:, :w] + cum2[d][:, w:]
            l_first = cum[first_end[d]:first_end[d] + 1, :]
            l_second = cum[second_end[d]:second_end[d] + 1, :]
            l_blk = l_first + l_second
            cum_blk = cum + jnp.where(first_half[d], 0.0, l_first)
            e = cum_blk - l_first
            qe.append((qb * jnp.exp(e)).astype(BF16))
            ke.append((kb * jnp.exp(-e)).astype(BF16))
            qs.append((qb * jnp.exp(cum_blk)).astype(BF16))
            kend.append((kb * jnp.exp(l_blk - cum_blk)).astype(BF16))
            dec.append(jnp.exp(l_blk))
        sls = [slice(h * dh, (h + 1) * dh) for h in range(hp)]
        a = [_bdot_nt(qe[d][:, sls[h]], ke[d][:, sls[h]]) for d, h in units]
        st = [st_s[d * hp + h] for d, h in units]
        inter = [_bdot_nt(qs[d][:, sls[h]], st[u]) for u, (d, h) in enumerate(units)]
        ds = [_bdot_tn(vb[d][:, sls[h]], kend[d][:, sls[h]]) for d, h in units]
        intra = [_bdot(jnp.where(attend[d], a[u], 0.0), vb[d][:, sls[h]]) for u, (d, h) in enumerate(units)]
        for u, (d, h) in enumerate(units):
            st_s[d * hp + h] = st[u] * dec[d][:, sls[h]] + ds[u]
            o_s[d, rows[d], sls[h]] = intra[u] + inter[u]
        return carry

    lax.fori_loop(0, nblk, block, 0, unroll=unroll)

    def finish(c, carry):
        rows = pl.ds(pl.multiple_of(c * HG_BLOCK, HG_BLOCK), HG_BLOCK)
        o = o_s[0, rows, :] + o_s[1, rows, :]
        gate = _silu(zg[rows, :])
        for h in range(hp):
            sl = slice(h * dh, (h + 1) * dh)
            oh = o[:, sl]
            ms = jnp.mean(oh * oh, axis=-1, keepdims=True)
            ya_ref[rows, sl] = (oh * lax.rsqrt(ms + NORM_EPS) * og_ref[...] * gate[:, sl]).astype(ya_ref.dtype)
        return carry

    lax.fori_loop(0, nblk, finish, 0)
    if has_sfin:
        for d in range(2):
            for h in range(hp):
                sfin_ref[0, 0, d, h] = st_s[d * hp + h].T


def _hgrn(z, lb_l, og_l, *, row_blk0, nb, seq, heads, dh, layer, s0=None, want_state=False, riders=(), hp=None,
          unroll=2):
    if hp is None:
        hp = next(c for c in (4, 2, 1) if heads % c == 0)
    w = hp * dh
    hw = heads * dh
    ng = heads // hp
    cb = hw // w

    def zspec(k):
        return pl.BlockSpec((seq, w), lambda b, g, k=k: (row_blk0 + b, k * cb + g))

    in_specs = [zspec(0), zspec(1), zspec(2), zspec(3), zspec(4),
                pl.BlockSpec((2, w), lambda b, g: (0, g)),
                pl.BlockSpec((1, dh), lambda b, g: (0, 0))]
    args = [z, z, z, z, z, lb_l, og_l.reshape(1, dh)]
    if s0 is not None:
        in_specs.append(pl.BlockSpec((1, 1, 2, hp, dh, dh), lambda b, g: (b, layer, 0, g, 0, 0)))
        args.append(s0)
    out_specs = [pl.BlockSpec((seq, w), lambda b, g: (b, g))]
    out_shape = [jax.ShapeDtypeStruct((nb * seq, hw), BF16)]
    if want_state:
        out_specs.append(pl.BlockSpec((1, 1, 2, hp, dh, dh), lambda b, g: (b, 0, 0, g, 0, 0)))
        out_shape.append(jax.ShapeDtypeStruct((nb, 1, 2, heads, dh, dh), F32))
    r_in, r_out, r_shape = _rider_specs(riders, nb * ng, lambda b, g: b * ng + g)
    res = pl.pallas_call(
        functools.partial(_hgrn_body, seq=seq, hp=hp, dh=dh, has_s0=s0 is not None, has_sfin=want_state,
                          n_riders=len(riders), unroll=unroll),
        grid=(nb, ng),
        in_specs=in_specs + r_in,
        out_specs=out_specs + r_out,
        out_shape=out_shape + r_shape,
        scratch_shapes=[
            pltpu.VMEM((2, seq, w), F32),
            pltpu.VMEM((2 * hp, dh, dh), F32),
        ],
        compiler_params=_cparams(("parallel", "parallel")),
        name="hgrn",
    )(*args, *riders)
    n_main = len(out_shape)
    return res[:n_main], res[n_main:]


def _head_rms(x, g):
    ms = jnp.mean(x * x, axis=-1, keepdims=True)
    return x * lax.rsqrt(ms + NORM_EPS) * g


def _ctx_attn_body(zq, zk, zv, qg_ref, kg_ref, yb_ref, k_ref, v_ref, *, hp, dh):
    scale = dh ** -0.5
    for h in range(hp):
        sl = slice(h * dh, (h + 1) * dh)
        q = _head_rms(zq[:, sl], qg_ref[...])
        k = _head_rms(zk[:, sl], kg_ref[...])
        v = zv[:, sl]
        k_ref[0, 0, h] = k
        v_ref[0, 0, h] = v
        s = _bdot_nt(q, k) * scale
        p = jnp.exp(s - jnp.max(s, axis=-1, keepdims=True))
        o = _bdot(p, v) / jnp.sum(p, axis=-1, keepdims=True)
        yb_ref[:, sl] = o.astype(yb_ref.dtype)


def _ctx_attention(z, qg, kg, *, nb, seq, heads, dh, col0):
    hp = 2 if heads % 2 == 0 else 1
    w = hp * dh
    nw = heads * dh
    ng = heads // hp
    c0 = col0 // w
    cb = nw // w

    def zspec(k):
        return pl.BlockSpec((seq, w), lambda b, g, k=k: (b, c0 + k * cb + g))

    kv_spec = pl.BlockSpec((1, 1, hp, seq, dh), lambda b, g: (b, 0, g, 0, 0))
    kv_shape = jax.ShapeDtypeStruct((nb, 1, heads, seq, dh), F32)
    return pl.pallas_call(
        functools.partial(_ctx_attn_body, hp=hp, dh=dh),
        grid=(nb, ng),
        in_specs=[zspec(0), zspec(1), zspec(2),
                  pl.BlockSpec((1, dh), lambda b, g: (0, 0)), pl.BlockSpec((1, dh), lambda b, g: (0, 0))],
        out_specs=[pl.BlockSpec((seq, w), lambda b, g: (b, g)), kv_spec, kv_spec],
        out_shape=[jax.ShapeDtypeStruct((nb * seq, nw), BF16), kv_shape, kv_shape],
        compiler_params=_cparams(("parallel", "parallel")),
        name="ctx_attention",
    )(z, z, z, qg.reshape(1, dh), kg.reshape(1, dh))


def _na_bias(rpb, seq):
    win_r, win_c = (rpb.shape[2] + 1) // 2, (rpb.shape[3] + 1) // 2
    rows = seq // GRID_W
    wr = min(win_r, rows)
    r, c = np.arange(rows), np.arange(GRID_W)
    r0 = np.clip(r - wr // 2, 0, rows - wr)
    c0 = np.clip(c - win_c // 2, 0, GRID_W - win_c)
    row_ok = (r[None, :] >= r0[:, None]) & (r[None, :] < r0[:, None] + wr)
    col_ok = (c[None, :] >= c0[:, None]) & (c[None, :] < c0[:, None] + win_c)
    valid = (row_ok[:, None, :, None] & col_ok[None, :, None, :]).reshape(seq, seq)
    sel_r = (r[None, :, None] - r[:, None, None] + win_r - 1 == np.arange(2 * win_r - 1)).astype(np.float32)
    sel_c = (c[None, :, None] - c[:, None, None] + win_c - 1 == np.arange(2 * win_c - 1)).astype(np.float32)
    table = jnp.einsum("xka,lhab,cqb->lhxckq", sel_r, rpb.astype(F32), sel_c, precision=lax.Precision.HIGHEST)
    table = table.reshape(rpb.shape[0], rpb.shape[1], seq, seq)
    return jnp.where(valid[None, None], table, NEG_INF)


def _na_attn_body(zq, zk, zv, kc_ref, vc_ref, bias_ref, qg_ref, kg_ref, yb_ref, *, seq, dh, tq):
    scale = dh ** -0.5
    q = _head_rms(zq[...], qg_ref[...]).astype(BF16)
    k = _head_rms(zk[...], kg_ref[...]).astype(BF16)
    v = zv[...].astype(BF16)
    kc = kc_ref[0, 0, 0].astype(BF16)
    vc = vc_ref[0, 0, 0].astype(BF16)
    for i in range(seq // tq):
        rows = slice(i * tq, (i + 1) * tq)
        qt = q[rows]
        s_win = _bdot_nt(qt, k) * scale + bias_ref[0, rows, :]
        s_ctx = _bdot_nt(qt, kc) * scale
        m = jnp.maximum(jnp.max(s_win, axis=-1, keepdims=True), jnp.max(s_ctx, axis=-1, keepdims=True))
        p_win = jnp.exp(s_win - m)
        p_ctx = jnp.exp(s_ctx - m)
        den = jnp.sum(p_win, axis=-1, keepdims=True) + jnp.sum(p_ctx, axis=-1, keepdims=True)
        o = (_bdot(p_win, v) + _bdot(p_ctx, vc)) / den
        yb_ref[rows, :] = o.astype(yb_ref.dtype)


def _na_attention(z, cache_k, cache_v, bias, qg, kg, *, row_blk0, nb, seq, heads, dh, col0, layer):
    nw = heads * dh
    c0 = col0 // dh
    past = cache_k.shape[3]

    def zspec(k):
        return pl.BlockSpec((seq, dh), lambda h, b, k=k: (row_blk0 + b, c0 + k * heads + h))

    cache_spec = pl.BlockSpec((1, 1, 1, past, dh), lambda h, b: (b, layer, h, 0, 0))
    return pl.pallas_call(
        functools.partial(_na_attn_body, seq=seq, dh=dh, tq=min(256, seq)),
        grid=(heads, nb),
        in_specs=[zspec(0), zspec(1), zspec(2), cache_spec, cache_spec,
                  pl.BlockSpec((1, seq, seq), lambda h, b: (h, 0, 0)),
                  pl.BlockSpec((1, dh), lambda h, b: (0, 0)), pl.BlockSpec((1, dh), lambda h, b: (0, 0))],
        out_specs=pl.BlockSpec((seq, dh), lambda h, b: (b, h)),
        out_shape=jax.ShapeDtypeStruct((nb * seq, nw), BF16),
        compiler_params=_cparams(("parallel", "parallel")),
        name="na_attention",
    )(z, z, z, cache_k, cache_v, bias, qg.reshape(1, dh), kg.reshape(1, dh))


def _merge_body(yap_ref, yas_ref, ybp_ref, ybs_ref, wh_ref, wn_ref, ga_ref, gb_ref, o_ref, *, npt):
    is_prompt = pl.program_id(0) < npt
    ya = jnp.where(is_prompt, yap_ref[...], yas_ref[...])
    yb = jnp.where(is_prompt, ybp_ref[...], ybs_ref[...])
    a = jnp.dot(ya, wh_ref[...], preferred_element_type=F32)
    b = jnp.dot(yb, wn_ref[...], preferred_element_type=F32)
    o_ref[...] = (jax.nn.sigmoid(ga_ref[...]) * a + jax.nn.sigmoid(gb_ref[...]) * b).astype(o_ref.dtype)


def _merge(ya, yb, w_hb, w_nb, z, col_ga, n_prompt, ts):
    n = z.shape[0]
    hw, nw = ya[0].shape[1], yb[0].shape[1]
    d = w_hb.shape[1]
    tm, tn = min(512, ts), _pick(d, 1024)
    npt = n_prompt // tm
    ca, cbk = col_ga // tn, (col_ga + d) // tn

    def split(width):
        return [pl.BlockSpec((tm, width), lambda i, j: (jnp.minimum(i, npt - 1), 0)),
                pl.BlockSpec((tm, width), lambda i, j: (jnp.maximum(i - npt, 0), 0))]

    return pl.pallas_call(
        functools.partial(_merge_body, npt=npt),
        grid=(n // tm, d // tn),
        in_specs=split(hw) + split(nw) + [
            pl.BlockSpec((hw, tn), lambda i, j: (0, j)),
            pl.BlockSpec((nw, tn), lambda i, j: (0, j)),
            pl.BlockSpec((tm, tn), lambda i, j: (i, ca + j)),
            pl.BlockSpec((tm, tn), lambda i, j: (i, cbk + j)),
        ],
        out_specs=pl.BlockSpec((tm, tn), lambda i, j: (i, j)),
        out_shape=jax.ShapeDtypeStruct((n, d), BF16),
        compiler_params=_cparams(("parallel", "parallel")),
        name="merge",
    )(*ya, *yb, w_hb, w_nb, z, z)


def _outproj_body(*refs, n_experts, n_stream, npt):
    y = _stream_tile(refs[:n_stream], npt)
    m_ref, w_ref, mod_ref, g_ref = refs[n_stream:n_stream + 4]
    if n_experts:
        r_ref, y1_ref, h2_ref, gw_ref, gi_ref = refs[n_stream + 4:]
    else:
        y1_ref, h2_ref = refs[n_stream + 4:]
    y1 = y + mod_ref[0, 2:3, :] * jnp.dot(m_ref[...], w_ref[...], preferred_element_type=F32)
    y1_ref[...] = y1
    h2 = _rms_modulate(y1, g_ref[...], mod_ref[0, 3:4, :], mod_ref[0, 4:5, :])
    h2_ref[...] = h2.astype(h2_ref.dtype)
    if n_experts:
        r = r_ref[...]
        h_hi = h2.astype(BF16)
        h_lo = (h2 - h_hi.astype(F32)).astype(BF16)
        r_hi = r.astype(BF16)
        r_lo = (r - r_hi.astype(F32)).astype(BF16)
        logits = (jnp.dot(h_hi, r_hi, preferred_element_type=F32)
                  + jnp.dot(h_hi, r_lo, preferred_element_type=F32)
                  + jnp.dot(h_lo, r_hi, preferred_element_type=F32))
        lane = lax.broadcasted_iota(jnp.int32, logits.shape, 1).astype(F32)
        big = float(LANES)
        lg = jnp.where(lane < n_experts, logits, -jnp.inf)
        m1 = jnp.max(lg, axis=-1, keepdims=True)
        i1 = jnp.min(jnp.where(lg == m1, lane, big), axis=-1, keepdims=True)
        lg2 = jnp.where(lane == i1, -jnp.inf, lg)
        m2 = jnp.max(lg2, axis=-1, keepdims=True)
        i2 = jnp.min(jnp.where(lg2 == m2, lane, big), axis=-1, keepdims=True)
        t = jnp.exp(m2 - m1)
        w1 = 1.0 / (1.0 + t)
        w2 = t / (1.0 + t)
        gw_ref[...] = jnp.where(lane == 0.0, w1, jnp.where(lane == 1.0, w2, 0.0))
        gi_ref[...] = jnp.where(lane == 0.0, i1, jnp.where(lane == 1.0, i2, 0.0)).astype(jnp.int32)


def _outproj(m, w_out, y, mod_l, g2, n_prompt, ts, router=None):
    n, d = m.shape
    tm = min(256, ts)
    n_experts = 0 if router is None else router.shape[1]
    y_specs, y_args = _stream_specs(y, tm, n_prompt)
    in_specs = y_specs + [
        pl.BlockSpec((tm, d), lambda j: (j, 0)),
        pl.BlockSpec((d, d), lambda j: (0, 0)),
        pl.BlockSpec((1, 6, d), lambda j: (_cond_of_rows(j * tm, n_prompt, ts), 0, 0)),
        pl.BlockSpec((1, d), lambda j: (0, 0)),
    ]
    args = y_args + [m, w_out, mod_l, g2.reshape(1, d)]
    row_spec = pl.BlockSpec((tm, d), lambda j: (j, 0))
    out_specs = [row_spec, row_spec]
    out_shape = [jax.ShapeDtypeStruct((n, d), F32), jax.ShapeDtypeStruct((n, d), F32 if n_experts else BF16)]
    if n_experts:
        in_specs.append(pl.BlockSpec((d, LANES), lambda j: (0, 0)))
        args.append(jnp.pad(router, ((0, 0), (0, LANES - n_experts))))
        lane_spec = pl.BlockSpec((tm, LANES), lambda j: (j, 0))
        out_specs += [lane_spec, lane_spec]
        out_shape += [jax.ShapeDtypeStruct((n, LANES), F32), jax.ShapeDtypeStruct((n, LANES), jnp.int32)]
    return pl.pallas_call(
        functools.partial(_outproj_body, n_experts=n_experts, n_stream=len(y_args), npt=n_prompt // tm),
        grid=(n // tm,),
        in_specs=in_specs,
        out_specs=out_specs,
        out_shape=out_shape,
        compiler_params=_cparams(("parallel",)),
        name="outproj",
    )(*args)


def _ffn_body(te_ref, nt_ref, *refs, residual):
    del te_ref
    if residual:
        x_ref, wg_ref, wu_ref, wd_ref, y_ref, mod_ref, o_ref = refs
    else:
        x_ref, wg_ref, wu_ref, wd_ref, o_ref = refs
    j, f = pl.program_id(0), pl.program_id(1)
    nf = pl.num_programs(1)

    @pl.when(j < nt_ref[0])
    def _():
        x = x_ref[...].astype(BF16)
        g = jnp.dot(x, wg_ref[0], preferred_element_type=F32)
        u = jnp.dot(x, wu_ref[0], preferred_element_type=F32)
        part = jnp.dot((_silu(g) * u).astype(BF16), wd_ref[0], preferred_element_type=F32)

        @pl.when(f == 0)
        def _():
            o_ref[...] = part

        @pl.when(f > 0)
        def _():
            o_ref[...] += part

        if residual:
            @pl.when(f == nf - 1)
            def _():
                o_ref[...] = y_ref[...] + mod_ref[0, 5:6, :] * o_ref[...]

    @pl.when((j >= nt_ref[0]) & (f == 0))
    def _():
        o_ref[...] = jnp.zeros_like(o_ref)


def _ffn(x, wg, wu, wd, tile_expert, n_tiles, *, tm, tf, residual=None):
    n, d = x.shape
    ff = wg.shape[2]
    tf = _pick(ff, tf)
    nf = ff // tf

    def row_map(j, f, te, nt):
        return (jnp.minimum(j, nt[0] - 1), 0)

    def ff_idx(j, f, nt):
        return jnp.where(j < nt[0], f, nf - 1)

    def e_idx(j, te, nt):
        return te[jnp.minimum(j, nt[0] - 1)]

    in_specs = [
        pl.BlockSpec((tm, d), row_map),
        pl.BlockSpec((1, d, tf), lambda j, f, te, nt: (e_idx(j, te, nt), 0, ff_idx(j, f, nt))),
        pl.BlockSpec((1, d, tf), lambda j, f, te, nt: (e_idx(j, te, nt), 0, ff_idx(j, f, nt))),
        pl.BlockSpec((1, tf, d), lambda j, f, te, nt: (e_idx(j, te, nt), ff_idx(j, f, nt), 0)),
    ]
    args = [x, wg, wu, wd]
    if residual is not None:
        y, mod_l, n_prompt, ts = residual
        in_specs += [
            pl.BlockSpec((tm, d), row_map),
            pl.BlockSpec((1, 6, d), lambda j, f, te, nt: (_cond_of_rows(j * tm, n_prompt, ts), 0, 0)),
        ]
        args += [y, mod_l]
    return pl.pallas_call(
        functools.partial(_ffn_body, residual=residual is not None),
        grid_spec=pltpu.PrefetchScalarGridSpec(
            num_scalar_prefetch=2,
            grid=(n // tm, nf),
            in_specs=in_specs,
            out_specs=pl.BlockSpec((tm, d), lambda j, f, te, nt: (j, 0)),
        ),
        out_shape=jax.ShapeDtypeStruct((n, d), F32),
        compiler_params=_cparams(("arbitrary", "arbitrary")),
        name="swiglu",
    )(tile_expert, n_tiles, *args)


def _dispatch_body(fill_ref, p0_ref, p1_ref, src, dst, zero_s, sem, zsem, *, tb, tm):
    @pl.when(pl.program_id(0) == 0)
    def _():
        zero_s[...] = jnp.zeros_like(zero_s)

        def fill(u):
            row0 = pl.multiple_of(jnp.maximum(fill_ref[u], 0) * tm, tm)
            return pltpu.make_async_copy(zero_s, dst.at[pl.ds(row0, tm)], zsem)

        for u in range(fill_ref.shape[0]):
            pl.when(fill_ref[u] >= 0)(fill(u).start)
        for u in range(fill_ref.shape[0]):
            pl.when(fill_ref[u] >= 0)(fill(u).wait)

    def issue(t, c):
        row = src.at[pl.ds(t, 1)]
        pltpu.make_async_copy(row, dst.at[pl.ds(p0_ref[0, 0, t], 1)], sem).start()
        pltpu.make_async_copy(row, dst.at[pl.ds(p1_ref[0, 0, t], 1)], sem).start()
        return c

    lax.fori_loop(0, tb, issue, 0)
    for _ in range(2):
        pltpu.make_async_copy(src, dst.at[pl.ds(0, tb)], sem).wait()


def _dispatch(h, pos0, pos1, fill_tiles, n_sorted, tb, tm):
    n, d = h.shape
    return pl.pallas_call(
        functools.partial(_dispatch_body, tb=tb, tm=tm),
        grid=(n // tb,),
        in_specs=[
            pl.BlockSpec(memory_space=pltpu.SMEM),
            pl.BlockSpec((1, 1, tb), lambda i: (i, 0, 0), memory_space=pltpu.SMEM),
            pl.BlockSpec((1, 1, tb), lambda i: (i, 0, 0), memory_space=pltpu.SMEM),
            pl.BlockSpec((tb, d), lambda i: (i, 0)),
        ],
        out_specs=pl.BlockSpec(memory_space=pl.ANY),
        out_shape=jax.ShapeDtypeStruct((n_sorted, d), h.dtype),
        scratch_shapes=[pltpu.VMEM((tm, d), h.dtype), pltpu.SemaphoreType.DMA(()), pltpu.SemaphoreType.DMA(())],
        compiler_params=_cparams(("arbitrary",)),
        name="dispatch",
    )(fill_tiles, pos0.reshape(n // tb, 1, tb), pos1.reshape(n // tb, 1, tb), h)


def _combine_body(p0_ref, p1_ref, p0n_ref, p1n_ref, o_hbm, y_ref, gw_ref, mod_ref, outp_ref, outs_ref, buf, sems,
                  *, tc, npt):
    i = pl.program_id(0)
    slot = i % 2

    def gather(pa_ref, pb_ref, s):
        def issue(t, c):
            pltpu.make_async_copy(o_hbm.at[pl.ds(pa_ref[0, 0, t], 1)], buf.at[s, 0, pl.ds(t, 1)], sems.at[s]).start()
            pltpu.make_async_copy(o_hbm.at[pl.ds(pb_ref[0, 0, t], 1)], buf.at[s, 1, pl.ds(t, 1)], sems.at[s]).start()
            return c

        lax.fori_loop(0, tc, issue, 0)

    @pl.when(i == 0)
    def _():
        gather(p0_ref, p1_ref, slot)

    @pl.when(i + 1 < pl.num_programs(0))
    def _():
        gather(p0n_ref, p1n_ref, 1 - slot)

    for k in range(2):
        pltpu.make_async_copy(o_hbm.at[pl.ds(0, tc)], buf.at[slot, k], sems.at[slot]).wait()
    gw = gw_ref[...]
    mix = gw[:, 0:1] * buf[slot, 0] + gw[:, 1:2] * buf[slot, 1]
    res = y_ref[...] + mod_ref[0, 5:6, :] * mix

    @pl.when(i < npt)
    def _():
        outp_ref[...] = res

    @pl.when(i >= npt)
    def _():
        outs_ref[...] = res


def _combine(o_sorted, pos0, pos1, y, gw, mod_l, n_prompt, ts):
    n, d = y.shape
    tc = min(256, ts)
    npt = n_prompt // tc
    steps = n // tc
    pos0, pos1 = pos0.reshape(steps, 1, tc), pos1.reshape(steps, 1, tc)
    pos_spec = pl.BlockSpec((1, 1, tc), lambda i: (i, 0, 0), memory_space=pltpu.SMEM)
    next_spec = pl.BlockSpec((1, 1, tc), lambda i: (jnp.minimum(i + 1, steps - 1), 0, 0), memory_space=pltpu.SMEM)
    return pl.pallas_call(
        functools.partial(_combine_body, tc=tc, npt=npt),
        grid=(steps,),
        in_specs=[
            pos_spec, pos_spec, next_spec, next_spec,
            pl.BlockSpec(memory_space=pl.ANY),
            pl.BlockSpec((tc, d), lambda i: (i, 0)),
            pl.BlockSpec((tc, LANES), lambda i: (i, 0)),
            pl.BlockSpec((1, 6, d), lambda i: (_cond_of_rows(i * tc, n_prompt, ts), 0, 0)),
        ],
        out_specs=[pl.BlockSpec((tc, d), lambda i: (jnp.minimum(i, npt - 1), 0)),
                   pl.BlockSpec((tc, d), lambda i: (jnp.maximum(i - npt, 0), 0))],
        out_shape=[jax.ShapeDtypeStruct((n_prompt, d), F32), jax.ShapeDtypeStruct((n - n_prompt, d), F32)],
        scratch_shapes=[pltpu.VMEM((2, 2, tc, d), F32), pltpu.SemaphoreType.DMA((2,))],
        compiler_params=_cparams(("arbitrary",)),
        name="combine",
    )(pos0, pos1, pos0, pos1, o_sorted, y, gw, mod_l)


def _routing_tables(top_i, n_experts, tm, n_tiles_max):
    n = top_i.shape[0]
    flat = jnp.concatenate([top_i[:, 0], top_i[:, 1]])
    onehot = (flat[:, None] == jnp.arange(n_experts, dtype=jnp.int32)[None, :]).astype(jnp.int32)
    csum = jnp.cumsum(onehot, axis=0)
    rank = jnp.sum((csum - onehot) * onehot, axis=1)
    counts = csum[-1]
    tiles = (counts + tm - 1) // tm
    tile_end = jnp.cumsum(tiles)
    start = (tile_end - tiles) * tm
    pos = jnp.sum(onehot * start[None, :], axis=1) + rank
    n_tiles = tile_end[-1]
    j = jnp.arange(n_tiles_max, dtype=jnp.int32)
    tile_expert = jnp.sum((j[:, None] >= tile_end[None, :]).astype(jnp.int32), axis=1)
    tile_expert = jnp.minimum(tile_expert, n_experts - 1).astype(jnp.int32)
    group_last = jnp.where(tiles > 0, tile_end - 1, -1)
    tail = n_tiles + jnp.arange(n_experts, dtype=jnp.int32)
    tail = jnp.where(tail < n_tiles_max, tail, -1)
    fill_tiles = jnp.concatenate([group_last, tail]).astype(jnp.int32)
    return (pos[:n].astype(jnp.int32), pos[n:].astype(jnp.int32), tile_expert,
            n_tiles.reshape(1).astype(jnp.int32), fill_tiles)


def kernel(x_prompt, x_sample, cache_k, cache_v, state_hgrn, c, c_ctx, norm1_g, norm2_g, w_ada, b_ada, w_in,
           hg_lb, hg_onorm_g, na_qn_g, na_kn_g, na_rpb, w_hb, w_nb, w_out, ffn_wg, ffn_wu, ffn_wd,
           moe_router, moe_wg, moe_wu, moe_wd):
    bp, tp, d = x_prompt.shape
    bs, ts, _ = x_sample.shape
    depth = w_in.shape[0]
    hg_heads, dh = state_hgrn.shape[3], state_hgrn.shape[4]
    na_heads = cache_k.shape[2]
    hw, nw = hg_heads * dh, na_heads * dh
    n_experts = moe_wg.shape[1]
    n_prompt, n_sample = bp * tp, bs * ts
    n = n_prompt + n_sample
    assert n_prompt % ts == 0 and ts % tp == 0 and bs + 1 <= N_COND
    col_nq = 5 * hw
    col_ga = 5 * hw + 3 * nw

    p = jax.nn.softmax(hg_lb.astype(F32), axis=1)
    cs = jnp.cumsum(p, axis=1)
    lbs = cs - cs[:, :1]
    bias = _na_bias(na_rpb, ts)
    ffn_b = tuple(a.astype(BF16) for a in (ffn_wg, ffn_wu, ffn_wd))
    n_ffe = moe_wg.shape[3]

    cond = jnp.concatenate([c_ctx[None, :], c, jnp.zeros((N_COND - 1 - bs, d), F32)], axis=0)
    mod = _modulation(cond, w_ada, b_ada).reshape(depth, N_COND, 6, d)

    y = (x_prompt.reshape(n_prompt, d), x_sample.reshape(n_sample, d))
    tm_moe = min(512, n)
    n_tiles_max = (2 * n) // tm_moe + n_experts
    ks_out, vs_out, ss_out = [], [], []
    moe_b = {}
    for l in range(depth):
        i = l // 2
        cast_next = l % 2 == 0 and l + 1 < depth
        if cast_next:
            in_riders = (moe_wg[i].reshape(n_experts * d, n_ffe),)
            hg_riders = (moe_wd[i].reshape(n_experts * n_ffe, d),)
        elif l % 2 == 1:
            in_riders, hg_riders = (moe_wu[i].reshape(n_experts * d, n_ffe),), ()
        else:
            in_riders, hg_riders = (), ()
        h = _prenorm(y, norm1_g[l], mod[l], n, n_prompt, ts)
        z, in_cast = _matmul(h, w_in[l].astype(BF16), in_riders)
        (ya_p, s_ctx), hg_cast = _hgrn(z, lbs[:, l], hg_onorm_g[l], row_blk0=0, nb=bp, seq=tp, heads=hg_heads,
                                       dh=dh, layer=l, want_state=True, riders=hg_riders)
        (ya_s,), _ = _hgrn(z, lbs[:, l], hg_onorm_g[l], row_blk0=n_prompt // ts, nb=bs, seq=ts, heads=hg_heads,
                           dh=dh, layer=l, s0=state_hgrn)
        if cast_next:
            moe_b["g"] = in_cast[0].reshape(n_experts, d, n_ffe)
            moe_b["d"] = hg_cast[0].reshape(n_experts, n_ffe, d)
        elif l % 2 == 1:
            moe_b["u"] = in_cast[0].reshape(n_experts, d, n_ffe)
        yb_p, k_new, v_new = _ctx_attention(z, na_qn_g[l], na_kn_g[l], nb=bp, seq=tp, heads=na_heads, dh=dh,
                                            col0=col_nq)
        yb_s = _na_attention(z, cache_k, cache_v, bias[l], na_qn_g[l], na_kn_g[l], row_blk0=n_prompt // ts,
                             nb=bs, seq=ts, heads=na_heads, dh=dh, col0=col_nq, layer=l)
        ks_out.append(k_new)
        vs_out.append(v_new)
        ss_out.append(s_ctx)
        m = _merge((ya_p, ya_s), (yb_p, yb_s), w_hb[l].astype(BF16), w_nb[l].astype(BF16), z, col_ga, n_prompt, ts)
        w_out_b = w_out[l].astype(BF16)
        if l % 2 == 0:
            y1, h2 = _outproj(m, w_out_b, y, mod[l], norm2_g[l], n_prompt, ts)
            tm = min(512, ts)
            y = _ffn(h2, ffn_b[0][i][None], ffn_b[1][i][None], ffn_b[2][i][None],
                     jnp.zeros((n // tm,), jnp.int32), jnp.full((1,), n // tm, jnp.int32), tm=tm, tf=512,
                     residual=(y1, mod[l], n_prompt, ts))
        else:
            y1, h2, gw, gi = _outproj(m, w_out_b, y, mod[l], norm2_g[l], n_prompt, ts, router=moe_router[i])
            pos0, pos1, tile_expert, n_tiles, fill_tiles = _routing_tables(gi[:, :2], n_experts, tm_moe,
                                                                          n_tiles_max)
            xs = _dispatch(h2, pos0, pos1, fill_tiles, n_tiles_max * tm_moe, min(1024, ts), tm_moe)
            o_sorted = _ffn(xs, moe_b["g"], moe_b["u"], moe_b["d"], tile_expert, n_tiles, tm=tm_moe, tf=1024)
            y = tuple(_combine(o_sorted, pos0, pos1, y1, gw, mod[l], n_prompt, ts))

    if not isinstance(y, tuple):
        y = (y[:n_prompt], y[n_prompt:])
    new_cache_k = jnp.concatenate(ks_out, axis=1)
    new_cache_v = jnp.concatenate(vs_out, axis=1)
    new_state_hgrn = jnp.concatenate(ss_out, axis=1)
    return (y[0].reshape(bp, tp, d), y[1].reshape(bs, ts, d), new_cache_k, new_cache_v, new_state_hgrn)
```

```python
import functools

import numpy as np
import jax
import jax.numpy as jnp
from jax import lax
from jax.experimental import pallas as pl
from jax.experimental.pallas import tpu as pltpu

F32 = jnp.float32
BF16 = jnp.bfloat16

NORM_EPS = 1e-6
NEG_INF = -1e30
GRID_W = 64
NA_QB = 16
LANES = 128
N_COND = 16
HG_BLOCK = 32
HG_HALF = HG_BLOCK // 2
VMEM_LIMIT = 56 * 1024 * 1024


def _cparams(semantics, vmem=VMEM_LIMIT):
    return pltpu.CompilerParams(dimension_semantics=semantics, vmem_limit_bytes=vmem)


def _silu(x):
    return x * jax.nn.sigmoid(x)


def _bdot(a, b):
    return jnp.dot(a.astype(BF16), b.astype(BF16), preferred_element_type=F32)


def _bdot_nt(a, b):
    return lax.dot_general(a.astype(BF16), b.astype(BF16), (((1,), (1,)), ((), ())),
                           preferred_element_type=F32)


def _bdot_tn(a, b):
    return lax.dot_general(a.astype(BF16), b.astype(BF16), (((0,), (0,)), ((), ())),
                           preferred_element_type=F32)


def _pick(n, pref):
    if n <= pref:
        return n
    t = pref - pref % LANES
    while n % t:
        t -= LANES
    return t


def _cond_of_rows(row0, n_prompt, ts):
    return jnp.where(row0 < n_prompt, 0, 1 + (row0 - n_prompt) // ts)


def _mod_body(c_ref, w_ref, b_ref, o_ref):
    o_ref[0] = _bdot(_silu(c_ref[...]), w_ref[0]) + b_ref[0]


def _modulation(cond, w_ada, b_ada):
    depth, d, n6 = w_ada.shape
    tn = _pick(n6, 1024)
    return pl.pallas_call(
        _mod_body,
        grid=(depth, n6 // tn),
        in_specs=[
            pl.BlockSpec((N_COND, d), lambda l, j: (0, 0)),
            pl.BlockSpec((1, d, tn), lambda l, j: (l, 0, j)),
            pl.BlockSpec((1, 1, tn), lambda l, j: (l, 0, j)),
        ],
        out_specs=pl.BlockSpec((1, N_COND, tn), lambda l, j: (l, 0, j)),
        out_shape=jax.ShapeDtypeStruct((depth, N_COND, n6), F32),
        compiler_params=_cparams(("parallel", "parallel")),
        name="modulation",
    )(cond, w_ada, b_ada.reshape(depth, 1, n6))


def _rms_modulate(x, g, shift, scale):
    ms = jnp.mean(x * x, axis=-1, keepdims=True)
    return (x * lax.rsqrt(ms + NORM_EPS) * g) * (1.0 + scale) + shift


def _stream_specs(y, tm, n_prompt):
    if not isinstance(y, tuple):
        return [pl.BlockSpec((tm, y.shape[1]), lambda j: (j, 0))], [y]
    d = y[0].shape[1]
    npt = n_prompt // tm
    return ([pl.BlockSpec((tm, d), lambda j: (jnp.minimum(j, npt - 1), 0)),
             pl.BlockSpec((tm, d), lambda j: (jnp.maximum(j - npt, 0), 0))], list(y))


def _stream_tile(refs, npt):
    if len(refs) == 1:
        return refs[0][...]
    return jnp.where(pl.program_id(0) < npt, refs[0][...], refs[1][...])


def _prenorm_body(*refs, npt):
    g_ref, m_ref, o_ref = refs[-3:]
    y = _stream_tile(refs[:-3], npt)
    o_ref[...] = _rms_modulate(y, g_ref[...], m_ref[0, 0:1, :], m_ref[0, 1:2, :]).astype(o_ref.dtype)


def _prenorm(y, g, mod_l, n, n_prompt, ts):
    d = g.shape[0]
    tm = min(512, ts)
    y_specs, y_args = _stream_specs(y, tm, n_prompt)
    return pl.pallas_call(
        functools.partial(_prenorm_body, npt=n_prompt // tm),
        grid=(n // tm,),
        in_specs=y_specs + [
            pl.BlockSpec((1, d), lambda j: (0, 0)),
            pl.BlockSpec((1, 6, d), lambda j: (_cond_of_rows(j * tm, n_prompt, ts), 0, 0)),
        ],
        out_specs=pl.BlockSpec((tm, d), lambda j: (j, 0)),
        out_shape=jax.ShapeDtypeStruct((n, d), BF16),
        compiler_params=_cparams(("parallel",)),
        name="prenorm",
    )(*y_args, g.reshape(1, d), mod_l)


def _rider_specs(riders, n_steps, step_of):
    in_specs, out_specs, out_shape = [], [], []
    for r in riders:
        rows, cols = r.shape
        assert rows % (n_steps * 16) == 0, (r.shape, n_steps)
        blk = (rows // n_steps, cols)
        in_specs.append(pl.BlockSpec(blk, lambda *g: (step_of(*g), 0)))
        out_specs.append(pl.BlockSpec(blk, lambda *g: (step_of(*g), 0)))
        out_shape.append(jax.ShapeDtypeStruct(r.shape, BF16))
    return in_specs, out_specs, out_shape


def _rider_copy(srcs, dsts):
    for src, dst in zip(srcs, dsts):
        dst[...] = src[...].astype(dst.dtype)


def _mm_body(x_ref, w_ref, *refs):
    n_r = (len(refs) - 1) // 2
    o_ref = refs[n_r]
    o_ref[...] = jnp.dot(x_ref[...], w_ref[...], preferred_element_type=F32).astype(o_ref.dtype)
    _rider_copy(refs[:n_r], refs[n_r + 1:])


def _matmul(x, w, riders=()):
    m, k = x.shape
    n = w.shape[1]
    tm, tn = _pick(m, 1024), _pick(n, 1536)
    gm, gn = m // tm, n // tn
    r_in, r_out, r_shape = _rider_specs(riders, gm * gn, lambda i, j: i * gn + j)
    res = pl.pallas_call(
        _mm_body,
        grid=(gm, gn),
        in_specs=[pl.BlockSpec((tm, k), lambda i, j: (i, 0)), pl.BlockSpec((k, tn), lambda i, j: (0, j))] + r_in,
        out_specs=[pl.BlockSpec((tm, tn), lambda i, j: (i, j))] + r_out,
        out_shape=[jax.ShapeDtypeStruct((m, n), F32)] + r_shape,
        compiler_params=_cparams(("parallel", "parallel")),
        name="in_proj",
    )(x, w, *riders)
    return res[0], res[1:]


def _hgrn_gates(z, lb):
    t = jnp.exp(-jnp.abs(z))
    pos = z >= 0.0
    inv = 1.0 / (1.0 + t)
    f = jnp.where(pos, 1.0 + lb * t, lb + t) * inv
    k = (1.0 - lb) * jnp.where(pos, t, 1.0) * inv
    return jnp.log(f), k


def _split2(x):
    hi = x.astype(BF16)
    lo = (x - hi.astype(F32)).astype(BF16)
    return hi, lo


def _hgrn_body(*refs, seq, hp, dh, has_s0, has_sfin, n_riders, unroll):
    zq, zff, zfb, zi, zg, lb_ref, og_ref = refs[:7]
    pos = 7
    s0_ref = None
    if has_s0:
        s0_ref = refs[pos]
        pos += 1
    r_src = refs[pos:pos + n_riders]
    pos += n_riders
    ya_ref = refs[pos]
    pos += 1
    sfin_ref = None
    if has_sfin:
        sfin_ref = refs[pos]
        pos += 1
    _rider_copy(r_src, refs[pos:pos + n_riders])
    pos += n_riders
    o_s, st_s = refs[pos:pos + 2]
    w = hp * dh
    nblk = seq // HG_BLOCK
    zf = (zff, zfb)

    for d in range(2):
        for h in range(hp):
            if has_s0:
                st_s[d * hp + h] = s0_ref[0, 0, d, h].T
            else:
                st_s[d * hp + h] = jnp.zeros((dh, dh), F32)

    row = lax.broadcasted_iota(jnp.int32, (HG_BLOCK, HG_BLOCK), 0)
    col = lax.broadcasted_iota(jnp.int32, (HG_BLOCK, HG_BLOCK), 1)
    same_half = (row // HG_HALF) == (col // HG_HALF)
    rowv = lax.broadcasted_iota(jnp.int32, (HG_BLOCK, 1), 0)
    attend = (col <= row, col >= row)
    seg = tuple(jnp.where(a & same_half, 1.0, 0.0).astype(BF16) for a in attend)
    first_half = (rowv < HG_HALF, rowv >= HG_HALF)
    first_end = (HG_HALF - 1, HG_HALF)
    second_end = (HG_BLOCK - 1, 0)

    units = [(d, h) for d in range(2) for h in range(hp)]

    def block(i, carry):
        rows = (pl.ds(pl.multiple_of(i * HG_BLOCK, HG_BLOCK), HG_BLOCK),
                pl.ds(pl.multiple_of(seq - HG_BLOCK * (i + 1), HG_BLOCK), HG_BLOCK))
        cum2, kbs = [], []
        for d in range(2):
            lf, kb = _hgrn_gates(zf[d][rows[d], :], lb_ref[d:d + 1, :])
            kbs.append(kb)
            hi, lo = _split2(lf)
            cum2.append(jnp.dot(seg[d], jnp.concatenate([hi, lo], axis=1), preferred_element_type=F32))
        vb = [zi[rows[d], :].astype(BF16) for d in range(2)]
        qe, ke, qs, kend, dec = [], [], [], [], []
        for d in range(2):
            qb = _silu(zq[rows[d], :])
            kb = kbs[d]
            cum = cum2[d][:, :w] + cum2[d][:, w:]
            l_first = cum[first_end[d]:first_end[d] + 1, :]
            l_second = cum[second_end[d]:second_end[d] + 1, :]
            l_blk = l_first + l_second
            cum_blk = cum + jnp.where(first_half[d], 0.0, l_first)
            e = cum_blk - l_first
            qe.append((qb * jnp.exp(e)).astype(BF16))
            ke.append((kb * jnp.exp(-e)).astype(BF16))
            qs.append((qb * jnp.exp(cum_blk)).astype(BF16))
            kend.append((kb * jnp.exp(l_blk - cum_blk)).astype(BF16))
            dec.append(jnp.exp(l_blk))
        sls = [slice(h * dh, (h + 1) * dh) for h in range(hp)]
        a = [_bdot_nt(qe[d][:, sls[h]], ke[d][:, sls[h]]) for d, h in units]
        st = [st_s[d * hp + h] for d, h in units]
        inter = [_bdot_nt(qs[d][:, sls[h]], st[u]) for u, (d, h) in enumerate(units)]
        ds = [_bdot_tn(vb[d][:, sls[h]], kend[d][:, sls[h]]) for d, h in units]
        intra = [_bdot(jnp.where(attend[d], a[u], 0.0), vb[d][:, sls[h]]) for u, (d, h) in enumerate(units)]
        for u, (d, h) in enumerate(units):
            st_s[d * hp + h] = st[u] * dec[d][:, sls[h]] + ds[u]
            o_s[d, rows[d], sls[h]] = intra[u] + inter[u]
        return carry

    lax.fori_loop(0, nblk, block, 0, unroll=unroll)

    def finish(c, carry):
        rows = pl.ds(pl.multiple_of(c * HG_BLOCK, HG_BLOCK), HG_BLOCK)
        o = o_s[0, rows, :] + o_s[1, rows, :]
        gate = _silu(zg[rows, :])
        for h in range(hp):
            sl = slice(h * dh, (h + 1) * dh)
            oh = o[:, sl]
            ms = jnp.mean(oh * oh, axis=-1, keepdims=True)
            ya_ref[rows, sl] = (oh * lax.rsqrt(ms + NORM_EPS) * og_ref[...] * gate[:, sl]).astype(ya_ref.dtype)
        return carry

    lax.fori_loop(0, nblk, finish, 0)
    if has_sfin:
        for d in range(2):
            for h in range(hp):
                sfin_ref[0, 0, d, h] = st_s[d * hp + h].T


def _hgrn(z, lb_l, og_l, *, row_blk0, nb, seq, heads, dh, layer, s0=None, want_state=False, riders=(), hp=None,
          unroll=4):
    if hp is None:
        hp = next(c for c in (4, 2, 1) if heads % c == 0)
    w = hp * dh
    hw = heads * dh
    ng = heads // hp
    cb = hw // w

    def zspec(k):
        return pl.BlockSpec((seq, w), lambda b, g, k=k: (row_blk0 + b, k * cb + g))

    in_specs = [zspec(0), zspec(1), zspec(2), zspec(3), zspec(4),
                pl.BlockSpec((2, w), lambda b, g: (0, g)),
                pl.BlockSpec((1, dh), lambda b, g: (0, 0))]
    args = [z, z, z, z, z, lb_l, og_l.reshape(1, dh)]
    if s0 is not None:
        in_specs.append(pl.BlockSpec((1, 1, 2, hp, dh, dh), lambda b, g: (b, layer, 0, g, 0, 0)))
        args.append(s0)
    out_specs = [pl.BlockSpec((seq, w), lambda b, g: (b, g))]
    out_shape = [jax.ShapeDtypeStruct((nb * seq, hw), BF16)]
    if want_state:
        out_specs.append(pl.BlockSpec((1, 1, 2, hp, dh, dh), lambda b, g: (b, 0, 0, g, 0, 0)))
        out_shape.append(jax.ShapeDtypeStruct((nb, 1, 2, heads, dh, dh), F32))
    r_in, r_out, r_shape = _rider_specs(riders, nb * ng, lambda b, g: b * ng + g)
    res = pl.pallas_call(
        functools.partial(_hgrn_body, seq=seq, hp=hp, dh=dh, has_s0=s0 is not None, has_sfin=want_state,
                          n_riders=len(riders), unroll=unroll),
        grid=(nb, ng),
        in_specs=in_specs + r_in,
        out_specs=out_specs + r_out,
        out_shape=out_shape + r_shape,
        scratch_shapes=[
            pltpu.VMEM((2, seq, w), F32),
            pltpu.VMEM((2 * hp, dh, dh), F32),
        ],
        compiler_params=_cparams(("parallel", "parallel")),
        name="hgrn",
    )(*args, *riders)
    n_main = len(out_shape)
    return res[:n_main], res[n_main:]


def _head_rms(x, g):
    ms = jnp.mean(x * x, axis=-1, keepdims=True)
    return x * lax.rsqrt(ms + NORM_EPS) * g


def _ctx_attn_body(zq, zk, zv, qg_ref, kg_ref, yb_ref, k_ref, v_ref, *, hp, dh):
    scale = dh ** -0.5
    for h in range(hp):
        sl = slice(h * dh, (h + 1) * dh)
        q = _head_rms(zq[:, sl], qg_ref[...])
        k = _head_rms(zk[:, sl], kg_ref[...])
        v = zv[:, sl]
        k_ref[0, 0, h] = k
        v_ref[0, 0, h] = v
        s = _bdot_nt(q, k) * scale
        p = jnp.exp(s - jnp.max(s, axis=-1, keepdims=True))
        o = _bdot(p, v) / jnp.sum(p, axis=-1, keepdims=True)
        yb_ref[:, sl] = o.astype(yb_ref.dtype)


def _ctx_attention(z, qg, kg, *, nb, seq, heads, dh, col0):
    hp = 2 if heads % 2 == 0 else 1
    w = hp * dh
    nw = heads * dh
    ng = heads // hp
    c0 = col0 // w
    cb = nw // w

    def zspec(k):
        return pl.BlockSpec((seq, w), lambda b, g, k=k: (b, c0 + k * cb + g))

    kv_spec = pl.BlockSpec((1, 1, hp, seq, dh), lambda b, g: (b, 0, g, 0, 0))
    kv_shape = jax.ShapeDtypeStruct((nb, 1, heads, seq, dh), F32)
    return pl.pallas_call(
        functools.partial(_ctx_attn_body, hp=hp, dh=dh),
        grid=(nb, ng),
        in_specs=[zspec(0), zspec(1), zspec(2),
                  pl.BlockSpec((1, dh), lambda b, g: (0, 0)), pl.BlockSpec((1, dh), lambda b, g: (0, 0))],
        out_specs=[pl.BlockSpec((seq, w), lambda b, g: (b, g)), kv_spec, kv_spec],
        out_shape=[jax.ShapeDtypeStruct((nb * seq, nw), BF16), kv_shape, kv_shape],
        compiler_params=_cparams(("parallel", "parallel")),
        name="ctx_attention",
    )(z, z, z, qg.reshape(1, dh), kg.reshape(1, dh))


def _na_bias(rpb, seq):
    win_r, win_c = (rpb.shape[2] + 1) // 2, (rpb.shape[3] + 1) // 2
    rows = seq // GRID_W
    wr = min(win_r, rows)
    r, c = np.arange(rows), np.arange(GRID_W)
    r0 = np.clip(r - wr // 2, 0, rows - wr)
    c0 = np.clip(c - win_c // 2, 0, GRID_W - win_c)
    row_ok = (r[None, :] >= r0[:, None]) & (r[None, :] < r0[:, None] + wr)
    col_ok = (c[None, :] >= c0[:, None]) & (c[None, :] < c0[:, None] + win_c)
    valid = (row_ok[:, None, :, None] & col_ok[None, :, None, :]).reshape(seq, seq)
    sel_r = (r[None, :, None] - r[:, None, None] + win_r - 1 == np.arange(2 * win_r - 1)).astype(np.float32)
    sel_c = (c[None, :, None] - c[:, None, None] + win_c - 1 == np.arange(2 * win_c - 1)).astype(np.float32)
    table = jnp.einsum("xka,lhab,cqb->lhxckq", sel_r, rpb.astype(F32), sel_c, precision=lax.Precision.HIGHEST)
    table = table.reshape(rpb.shape[0], rpb.shape[1], seq, seq)
    return jnp.where(valid[None, None], table, NEG_INF)


def _na_attn_body(zq, zk, zv, kc_ref, vc_ref, bias_ref, qg_ref, kg_ref, yb_ref, *, seq, dh, tq):
    scale = dh ** -0.5
    q = _head_rms(zq[...], qg_ref[...]).astype(BF16)
    k = _head_rms(zk[...], kg_ref[...]).astype(BF16)
    v = zv[...].astype(BF16)
    kc = kc_ref[0, 0, 0].astype(BF16)
    vc = vc_ref[0, 0, 0].astype(BF16)
    for i in range(seq // tq):
        rows = slice(i * tq, (i + 1) * tq)
        qt = q[rows]
        s_win = _bdot_nt(qt, k) * scale + bias_ref[0, rows, :]
        s_ctx = _bdot_nt(qt, kc) * scale
        m = jnp.maximum(jnp.max(s_win, axis=-1, keepdims=True), jnp.max(s_ctx, axis=-1, keepdims=True))
        p_win = jnp.exp(s_win - m)
        p_ctx = jnp.exp(s_ctx - m)
        den = jnp.sum(p_win, axis=-1, keepdims=True) + jnp.sum(p_ctx, axis=-1, keepdims=True)
        o = (_bdot(p_win, v) + _bdot(p_ctx, vc)) / den
        yb_ref[rows, :] = o.astype(yb_ref.dtype)


def _na_attention(z, cache_k, cache_v, bias, qg, kg, *, row_blk0, nb, seq, heads, dh, col0, layer):
    nw = heads * dh
    c0 = col0 // dh
    past = cache_k.shape[3]

    def zspec(k):
        return pl.BlockSpec((seq, dh), lambda h, b, k=k: (row_blk0 + b, c0 + k * heads + h))

    cache_spec = pl.BlockSpec((1, 1, 1, past, dh), lambda h, b: (b, layer, h, 0, 0))
    return pl.pallas_call(
        functools.partial(_na_attn_body, seq=seq, dh=dh, tq=min(256, seq)),
        grid=(heads, nb),
        in_specs=[zspec(0), zspec(1), zspec(2), cache_spec, cache_spec,
                  pl.BlockSpec((1, seq, seq), lambda h, b: (h, 0, 0)),
                  pl.BlockSpec((1, dh), lambda h, b: (0, 0)), pl.BlockSpec((1, dh), lambda h, b: (0, 0))],
        out_specs=pl.BlockSpec((seq, dh), lambda h, b: (b, h)),
        out_shape=jax.ShapeDtypeStruct((nb * seq, nw), BF16),
        compiler_params=_cparams(("parallel", "parallel")),
        name="na_attention",
    )(z, z, z, cache_k, cache_v, bias, qg.reshape(1, dh), kg.reshape(1, dh))


def _merge_body(yap_ref, yas_ref, ybp_ref, ybs_ref, wh_ref, wn_ref, ga_ref, gb_ref, o_ref, *, npt):
    is_prompt = pl.program_id(0) < npt
    ya = jnp.where(is_prompt, yap_ref[...], yas_ref[...])
    yb = jnp.where(is_prompt, ybp_ref[...], ybs_ref[...])
    a = jnp.dot(ya, wh_ref[...], preferred_element_type=F32)
    b = jnp.dot(yb, wn_ref[...], preferred_element_type=F32)
    o_ref[...] = (jax.nn.sigmoid(ga_ref[...]) * a + jax.nn.sigmoid(gb_ref[...]) * b).astype(o_ref.dtype)


def _merge(ya, yb, w_hb, w_nb, z, col_ga, n_prompt, ts):
    n = z.shape[0]
    hw, nw = ya[0].shape[1], yb[0].shape[1]
    d = w_hb.shape[1]
    tm, tn = min(512, ts), _pick(d, 1024)
    npt = n_prompt // tm
    ca, cbk = col_ga // tn, (col_ga + d) // tn

    def split(width):
        return [pl.BlockSpec((tm, width), lambda i, j: (jnp.minimum(i, npt - 1), 0)),
                pl.BlockSpec((tm, width), lambda i, j: (jnp.maximum(i - npt, 0), 0))]

    return pl.pallas_call(
        functools.partial(_merge_body, npt=npt),
        grid=(n // tm, d // tn),
        in_specs=split(hw) + split(nw) + [
            pl.BlockSpec((hw, tn), lambda i, j: (0, j)),
            pl.BlockSpec((nw, tn), lambda i, j: (0, j)),
            pl.BlockSpec((tm, tn), lambda i, j: (i, ca + j)),
            pl.BlockSpec((tm, tn), lambda i, j: (i, cbk + j)),
        ],
        out_specs=pl.BlockSpec((tm, tn), lambda i, j: (i, j)),
        out_shape=jax.ShapeDtypeStruct((n, d), BF16),
        compiler_params=_cparams(("parallel", "parallel")),
        name="merge",
    )(*ya, *yb, w_hb, w_nb, z, z)


def _outproj_body(*refs, n_experts, n_stream, npt):
    y = _stream_tile(refs[:n_stream], npt)
    m_ref, w_ref, mod_ref, g_ref = refs[n_stream:n_stream + 4]
    if n_experts:
        r_ref, y1_ref, h2_ref, gw_ref, gi_ref = refs[n_stream + 4:]
    else:
        y1_ref, h2_ref = refs[n_stream + 4:]
    y1 = y + mod_ref[0, 2:3, :] * jnp.dot(m_ref[...], w_ref[...], preferred_element_type=F32)
    y1_ref[...] = y1
    h2 = _rms_modulate(y1, g_ref[...], mod_ref[0, 3:4, :], mod_ref[0, 4:5, :])
    h2_ref[...] = h2.astype(h2_ref.dtype)
    if n_experts:
        r = r_ref[...]
        h_hi = h2.astype(BF16)
        h_lo = (h2 - h_hi.astype(F32)).astype(BF16)
        r_hi = r.astype(BF16)
        r_lo = (r - r_hi.astype(F32)).astype(BF16)
        logits = (jnp.dot(h_hi, r_hi, preferred_element_type=F32)
                  + jnp.dot(h_hi, r_lo, preferred_element_type=F32)
                  + jnp.dot(h_lo, r_hi, preferred_element_type=F32))
        lane = lax.broadcasted_iota(jnp.int32, logits.shape, 1).astype(F32)
        big = float(LANES)
        lg = jnp.where(lane < n_experts, logits, -jnp.inf)
        m1 = jnp.max(lg, axis=-1, keepdims=True)
        i1 = jnp.min(jnp.where(lg == m1, lane, big), axis=-1, keepdims=True)
        lg2 = jnp.where(lane == i1, -jnp.inf, lg)
        m2 = jnp.max(lg2, axis=-1, keepdims=True)
        i2 = jnp.min(jnp.where(lg2 == m2, lane, big), axis=-1, keepdims=True)
        t = jnp.exp(m2 - m1)
        w1 = 1.0 / (1.0 + t)
        w2 = t / (1.0 + t)
        gw_ref[...] = jnp.where(lane == 0.0, w1, jnp.where(lane == 1.0, w2, 0.0))
        gi_ref[...] = jnp.where(lane == 0.0, i1, jnp.where(lane == 1.0, i2, 0.0)).astype(jnp.int32)


def _outproj(m, w_out, y, mod_l, g2, n_prompt, ts, router=None):
    n, d = m.shape
    tm = min(256, ts)
    n_experts = 0 if router is None else router.shape[1]
    y_specs, y_args = _stream_specs(y, tm, n_prompt)
    in_specs = y_specs + [
        pl.BlockSpec((tm, d), lambda j: (j, 0)),
        pl.BlockSpec((d, d), lambda j: (0, 0)),
        pl.BlockSpec((1, 6, d), lambda j: (_cond_of_rows(j * tm, n_prompt, ts), 0, 0)),
        pl.BlockSpec((1, d), lambda j: (0, 0)),
    ]
    args = y_args + [m, w_out, mod_l, g2.reshape(1, d)]
    row_spec = pl.BlockSpec((tm, d), lambda j: (j, 0))
    out_specs = [row_spec, row_spec]
    out_shape = [jax.ShapeDtypeStruct((n, d), F32), jax.ShapeDtypeStruct((n, d), F32 if n_experts else BF16)]
    if n_experts:
        in_specs.append(pl.BlockSpec((d, LANES), lambda j: (0, 0)))
        args.append(jnp.pad(router, ((0, 0), (0, LANES - n_experts))))
        lane_spec = pl.BlockSpec((tm, LANES), lambda j: (j, 0))
        out_specs += [lane_spec, lane_spec]
        out_shape += [jax.ShapeDtypeStruct((n, LANES), F32), jax.ShapeDtypeStruct((n, LANES), jnp.int32)]
    return pl.pallas_call(
        functools.partial(_outproj_body, n_experts=n_experts, n_stream=len(y_args), npt=n_prompt // tm),
        grid=(n // tm,),
        in_specs=in_specs,
        out_specs=out_specs,
        out_shape=out_shape,
        compiler_params=_cparams(("parallel",)),
        name="outproj",
    )(*args)


def _ffn_body(te_ref, nt_ref, *refs, residual):
    del te_ref
    if residual:
        x_ref, wg_ref, wu_ref, wd_ref, y_ref, mod_ref, o_ref = refs
    else:
        x_ref, wg_ref, wu_ref, wd_ref, o_ref = refs
    j, f = pl.program_id(0), pl.program_id(1)
    nf = pl.num_programs(1)

    @pl.when(j < nt_ref[0])
    def _():
        x = x_ref[...].astype(BF16)
        g = jnp.dot(x, wg_ref[0], preferred_element_type=F32)
        u = jnp.dot(x, wu_ref[0], preferred_element_type=F32)
        part = jnp.dot((_silu(g) * u).astype(BF16), wd_ref[0], preferred_element_type=F32)

        @pl.when(f == 0)
        def _():
            o_ref[...] = part

        @pl.when(f > 0)
        def _():
            o_ref[...] += part

        if residual:
            @pl.when(f == nf - 1)
            def _():
                o_ref[...] = y_ref[...] + mod_ref[0, 5:6, :] * o_ref[...]

    @pl.when((j >= nt_ref[0]) & (f == 0))
    def _():
        o_ref[...] = jnp.zeros_like(o_ref)


def _ffn(x, wg, wu, wd, tile_expert, n_tiles, *, tm, tf, residual=None):
    n, d = x.shape
    ff = wg.shape[2]
    tf = _pick(ff, tf)
    nf = ff // tf

    def row_map(j, f, te, nt):
        return (jnp.minimum(j, nt[0] - 1), 0)

    def ff_idx(j, f, nt):
        return jnp.where(j < nt[0], f, nf - 1)

    def e_idx(j, te, nt):
        return te[jnp.minimum(j, nt[0] - 1)]

    in_specs = [
        pl.BlockSpec((tm, d), row_map),
        pl.BlockSpec((1, d, tf), lambda j, f, te, nt: (e_idx(j, te, nt), 0, ff_idx(j, f, nt))),
        pl.BlockSpec((1, d, tf), lambda j, f, te, nt: (e_idx(j, te, nt), 0, ff_idx(j, f, nt))),
        pl.BlockSpec((1, tf, d), lambda j, f, te, nt: (e_idx(j, te, nt), ff_idx(j, f, nt), 0)),
    ]
    args = [x, wg, wu, wd]
    if residual is not None:
        y, mod_l, n_prompt, ts = residual
        in_specs += [
            pl.BlockSpec((tm, d), row_map),
            pl.BlockSpec((1, 6, d), lambda j, f, te, nt: (_cond_of_rows(j * tm, n_prompt, ts), 0, 0)),
        ]
        args += [y, mod_l]
    return pl.pallas_call(
        functools.partial(_ffn_body, residual=residual is not None),
        grid_spec=pltpu.PrefetchScalarGridSpec(
            num_scalar_prefetch=2,
            grid=(n // tm, nf),
            in_specs=in_specs,
            out_specs=pl.BlockSpec((tm, d), lambda j, f, te, nt: (j, 0)),
        ),
        out_shape=jax.ShapeDtypeStruct((n, d), F32),
        compiler_params=_cparams(("arbitrary", "arbitrary")),
        name="swiglu",
    )(tile_expert, n_tiles, *args)


def _dispatch_body(fill_ref, p0_ref, p1_ref, src, dst, zero_s, sem, zsem, *, tb, tm):
    @pl.when(pl.program_id(0) == 0)
    def _():
        zero_s[...] = jnp.zeros_like(zero_s)

        def fill(u):
            row0 = pl.multiple_of(jnp.maximum(fill_ref[u], 0) * tm, tm)
            return pltpu.make_async_copy(zero_s, dst.at[pl.ds(row0, tm)], zsem)

        for u in range(fill_ref.shape[0]):
            pl.when(fill_ref[u] >= 0)(fill(u).start)
        for u in range(fill_ref.shape[0]):
            pl.when(fill_ref[u] >= 0)(fill(u).wait)

    def issue(t, c):
        row = src.at[pl.ds(t, 1)]
        pltpu.make_async_copy(row, dst.at[pl.ds(p0_ref[0, 0, t], 1)], sem).start()
        pltpu.make_async_copy(row, dst.at[pl.ds(p1_ref[0, 0, t], 1)], sem).start()
        return c

    lax.fori_loop(0, tb, issue, 0)
    for _ in range(2):
        pltpu.make_async_copy(src, dst.at[pl.ds(0, tb)], sem).wait()


def _dispatch(h, pos0, pos1, fill_tiles, n_sorted, tb, tm):
    n, d = h.shape
    return pl.pallas_call(
        functools.partial(_dispatch_body, tb=tb, tm=tm),
        grid=(n // tb,),
        in_specs=[
            pl.BlockSpec(memory_space=pltpu.SMEM),
            pl.BlockSpec((1, 1, tb), lambda i: (i, 0, 0), memory_space=pltpu.SMEM),
            pl.BlockSpec((1, 1, tb), lambda i: (i, 0, 0), memory_space=pltpu.SMEM),
            pl.BlockSpec((tb, d), lambda i: (i, 0)),
        ],
        out_specs=pl.BlockSpec(memory_space=pl.ANY),
        out_shape=jax.ShapeDtypeStruct((n_sorted, d), h.dtype),
        scratch_shapes=[pltpu.VMEM((tm, d), h.dtype), pltpu.SemaphoreType.DMA(()), pltpu.SemaphoreType.DMA(())],
        compiler_params=_cparams(("arbitrary",)),
        name="dispatch",
    )(fill_tiles, pos0.reshape(n // tb, 1, tb), pos1.reshape(n // tb, 1, tb), h)


def _combine_body(p0_ref, p1_ref, p0n_ref, p1n_ref, o_hbm, y_ref, gw_ref, mod_ref, outp_ref, outs_ref, buf, sems,
                  *, tc, npt):
    i = pl.program_id(0)
    slot = i % 2

    def gather(pa_ref, pb_ref, s):
        def issue(t, c):
            pltpu.make_async_copy(o_hbm.at[pl.ds(pa_ref[0, 0, t], 1)], buf.at[s, 0, pl.ds(t, 1)], sems.at[s]).start()
            pltpu.make_async_copy(o_hbm.at[pl.ds(pb_ref[0, 0, t], 1)], buf.at[s, 1, pl.ds(t, 1)], sems.at[s]).start()
            return c

        lax.fori_loop(0, tc, issue, 0)

    @pl.when(i == 0)
    def _():
        gather(p0_ref, p1_ref, slot)

    @pl.when(i + 1 < pl.num_programs(0))
    def _():
        gather(p0n_ref, p1n_ref, 1 - slot)

    for k in range(2):
        pltpu.make_async_copy(o_hbm.at[pl.ds(0, tc)], buf.at[slot, k], sems.at[slot]).wait()
    gw = gw_ref[...]
    mix = gw[:, 0:1] * buf[slot, 0] + gw[:, 1:2] * buf[slot, 1]
    res = y_ref[...] + mod_ref[0, 5:6, :] * mix

    @pl.when(i < npt)
    def _():
        outp_ref[...] = res

    @pl.when(i >= npt)
    def _():
        outs_ref[...] = res


def _combine(o_sorted, pos0, pos1, y, gw, mod_l, n_prompt, ts):
    n, d = y.shape
    tc = min(256, ts)
    npt = n_prompt // tc
    steps = n // tc
    pos0, pos1 = pos0.reshape(steps, 1, tc), pos1.reshape(steps, 1, tc)
    pos_spec = pl.BlockSpec((1, 1, tc), lambda i: (i, 0, 0), memory_space=pltpu.SMEM)
    next_spec = pl.BlockSpec((1, 1, tc), lambda i: (jnp.minimum(i + 1, steps - 1), 0, 0), memory_space=pltpu.SMEM)
    return pl.pallas_call(
        functools.partial(_combine_body, tc=tc, npt=npt),
        grid=(steps,),
        in_specs=[
            pos_spec, pos_spec, next_spec, next_spec,
            pl.BlockSpec(memory_space=pl.ANY),
            pl.BlockSpec((tc, d), lambda i: (i, 0)),
            pl.BlockSpec((tc, LANES), lambda i: (i, 0)),
            pl.BlockSpec((1, 6, d), lambda i: (_cond_of_rows(i * tc, n_prompt, ts), 0, 0)),
        ],
        out_specs=[pl.BlockSpec((tc, d), lambda i: (jnp.minimum(i, npt - 1), 0)),
                   pl.BlockSpec((tc, d), lambda i: (jnp.maximum(i - npt, 0), 0))],
        out_shape=[jax.ShapeDtypeStruct((n_prompt, d), F32), jax.ShapeDtypeStruct((n - n_prompt, d), F32)],
        scratch_shapes=[pltpu.VMEM((2, 2, tc, d), F32), pltpu.SemaphoreType.DMA((2,))],
        compiler_params=_cparams(("arbitrary",)),
        name="combine",
    )(pos0, pos1, pos0, pos1, o_sorted, y, gw, mod_l)


def _routing_tables(top_i, n_experts, tm, n_tiles_max):
    n = top_i.shape[0]
    flat = jnp.concatenate([top_i[:, 0], top_i[:, 1]])
    onehot = (flat[:, None] == jnp.arange(n_experts, dtype=jnp.int32)[None, :]).astype(jnp.int32)
    csum = jnp.cumsum(onehot, axis=0)
    rank = jnp.sum((csum - onehot) * onehot, axis=1)
    counts = csum[-1]
    tiles = (counts + tm - 1) // tm
    tile_end = jnp.cumsum(tiles)
    start = (tile_end - tiles) * tm
    pos = jnp.sum(onehot * start[None, :], axis=1) + rank
    n_tiles = tile_end[-1]
    j = jnp.arange(n_tiles_max, dtype=jnp.int32)
    tile_expert = jnp.sum((j[:, None] >= tile_end[None, :]).astype(jnp.int32), axis=1)
    tile_expert = jnp.minimum(tile_expert, n_experts - 1).astype(jnp.int32)
    group_last = jnp.where(tiles > 0, tile_end - 1, -1)
    tail = n_tiles + jnp.arange(n_experts, dtype=jnp.int32)
    tail = jnp.where(tail < n_tiles_max, tail, -1)
    fill_tiles = jnp.concatenate([group_last, tail]).astype(jnp.int32)
    return (pos[:n].astype(jnp.int32), pos[n:].astype(jnp.int32), tile_expert,
            n_tiles.reshape(1).astype(jnp.int32), fill_tiles)


def kernel(x_prompt, x_sample, cache_k, cache_v, state_hgrn, c, c_ctx, norm1_g, norm2_g, w_ada, b_ada, w_in,
           hg_lb, hg_onorm_g, na_qn_g, na_kn_g, na_rpb, w_hb, w_nb, w_out, ffn_wg, ffn_wu, ffn_wd,
           moe_router, moe_wg, moe_wu, moe_wd):
    bp, tp, d = x_prompt.shape
    bs, ts, _ = x_sample.shape
    depth = w_in.shape[0]
    hg_heads, dh = state_hgrn.shape[3], state_hgrn.shape[4]
    na_heads = cache_k.shape[2]
    hw, nw = hg_heads * dh, na_heads * dh
    n_experts = moe_wg.shape[1]
    n_prompt, n_sample = bp * tp, bs * ts
    n = n_prompt + n_sample
    assert n_prompt % ts == 0 and ts % tp == 0 and bs + 1 <= N_COND
    col_nq = 5 * hw
    col_ga = 5 * hw + 3 * nw

    p = jax.nn.softmax(hg_lb.astype(F32), axis=1)
    cs = jnp.cumsum(p, axis=1)
    lbs = cs - cs[:, :1]
    bias = _na_bias(na_rpb, ts)
    ffn_b = tuple(a.astype(BF16) for a in (ffn_wg, ffn_wu, ffn_wd))
    n_ffe = moe_wg.shape[3]

    cond = jnp.concatenate([c_ctx[None, :], c, jnp.zeros((N_COND - 1 - bs, d), F32)], axis=0)
    mod = _modulation(cond, w_ada, b_ada).reshape(depth, N_COND, 6, d)

    y = (x_prompt.reshape(n_prompt, d), x_sample.reshape(n_sample, d))
    tm_moe = min(512, n)
    n_tiles_max = (2 * n) // tm_moe + n_experts
    ks_out, vs_out, ss_out = [], [], []
    moe_b = {}
    for l in range(depth):
        i = l // 2
        cast_next = l % 2 == 0 and l + 1 < depth
        if cast_next:
            in_riders = (moe_wg[i].reshape(n_experts * d, n_ffe),)
            hg_riders = (moe_wd[i].reshape(n_experts * n_ffe, d),)
        elif l % 2 == 1:
            in_riders, hg_riders = (moe_wu[i].reshape(n_experts * d, n_ffe),), ()
        else:
            in_riders, hg_riders = (), ()
        h = _prenorm(y, norm1_g[l], mod[l], n, n_prompt, ts)
        z, in_cast = _matmul(h, w_in[l].astype(BF16), in_riders)
        (ya_p, s_ctx), hg_cast = _hgrn(z, lbs[:, l], hg_onorm_g[l], row_blk0=0, nb=bp, seq=tp, heads=hg_heads,
                                       dh=dh, layer=l, want_state=True, riders=hg_riders)
        (ya_s,), _ = _hgrn(z, lbs[:, l], hg_onorm_g[l], row_blk0=n_prompt // ts, nb=bs, seq=ts, heads=hg_heads,
                           dh=dh, layer=l, s0=state_hgrn)
        if cast_next:
            moe_b["g"] = in_cast[0].reshape(n_experts, d, n_ffe)
            moe_b["d"] = hg_cast[0].reshape(n_experts, n_ffe, d)
        elif l % 2 == 1:
            moe_b["u"] = in_cast[0].reshape(n_experts, d, n_ffe)
        yb_p, k_new, v_new = _ctx_attention(z, na_qn_g[l], na_kn_g[l], nb=bp, seq=tp, heads=na_heads, dh=dh,
                                            col0=col_nq)
        yb_s = _na_attention(z, cache_k, cache_v, bias[l], na_qn_g[l], na_kn_g[l], row_blk0=n_prompt // ts,
                             nb=bs, seq=ts, heads=na_heads, dh=dh, col0=col_nq, layer=l)
        ks_out.append(k_new)
        vs_out.append(v_new)
        ss_out.append(s_ctx)
        m = _merge((ya_p, ya_s), (yb_p, yb_s), w_hb[l].astype(BF16), w_nb[l].astype(BF16), z, col_ga, n_prompt, ts)
        w_out_b = w_out[l].astype(BF16)
        if l % 2 == 0:
            y1, h2 = _outproj(m, w_out_b, y, mod[l], norm2_g[l], n_prompt, ts)
            tm = min(512, ts)
            y = _ffn(h2, ffn_b[0][i][None], ffn_b[1][i][None], ffn_b[2][i][None],
                     jnp.zeros((n // tm,), jnp.int32), jnp.full((1,), n // tm, jnp.int32), tm=tm, tf=512,
                     residual=(y1, mod[l], n_prompt, ts))
        else:
            y1, h2, gw, gi = _outproj(m, w_out_b, y, mod[l], norm2_g[l], n_prompt, ts, router=moe_router[i])
            pos0, pos1, tile_expert, n_tiles, fill_tiles = _routing_tables(gi[:, :2], n_experts, tm_moe,
                                                                          n_tiles_max)
            xs = _dispatch(h2, pos0, pos1, fill_tiles, n_tiles_max * tm_moe, min(1024, ts), tm_moe)
            o_sorted = _ffn(xs, moe_b["g"], moe_b["u"], moe_b["d"], tile_expert, n_tiles, tm=tm_moe, tf=1024)
            y = tuple(_combine(o_sorted, pos0, pos1, y1, gw, mod[l], n_prompt, ts))

    if not isinstance(y, tuple):
        y = (y[:n_prompt], y[n_prompt:])
    new_cache_k = jnp.concatenate(ks_out, axis=1)
    new_cache_v = jnp.concatenate(vs_out, axis=1)
    new_state_hgrn = jnp.concatenate(ss_out, axis=1)
    return (y[0].reshape(bp, tp, d), y[1].reshape(bs, ts, d), new_cache_k, new_cache_v, new_state_hgrn)
```
